```python
import jax, jax.numpy as jnp
from jax import lax
import numpy as np

D_MODEL = 2048
BATCH = 8
SEQ = 2048
DEPTH = 4
DEC_BATCH = 2
DEC_SEQ = 8192
PAST_LEN = 128

PLE_DIM = 256
FNET_GROUPS = 4
FNET_WIDTH = D_MODEL // 4
FNET_GROUP_DIM = FNET_WIDTH // FNET_GROUPS
POOL_WINDOWS = (2, 4, 8, 16)
POOL_GROUPS = len(POOL_WINDOWS)
POOL_WIDTH = D_MODEL // 2
POOL_GROUP_DIM = POOL_WIDTH // POOL_GROUPS
CONV_WIDTH = D_MODEL // 4
CONV_KERNEL = 31
N_BRANCHES = 3
IN_SPLITS = (FNET_WIDTH, FNET_WIDTH,
             POOL_WIDTH, POOL_WIDTH,
             2 * CONV_WIDTH, CONV_WIDTH,
             N_BRANCHES * D_MODEL)
N_IN = sum(IN_SPLITS)
IN_OFFSETS = tuple(int(o) for o in np.cumsum(IN_SPLITS)[:-1])
DEEPNORM_ALPHA = (2.0 * DEPTH) ** 0.25
DEEPNORM_BETA = (8.0 * DEPTH) ** -0.25
LN_EPS = 1e-5

kernel_name = "hybrid_fnet_pool_conformer_encoder"


def layer_norm(x, g, b):
    x32 = x.astype(jnp.float32)
    mu = jnp.mean(x32, axis=-1, keepdims=True)
    var = jnp.mean(jnp.square(x32 - mu), axis=-1, keepdims=True)
    y = (x32 - mu) * lax.rsqrt(var + LN_EPS) * g.astype(jnp.float32) + b.astype(jnp.float32)
    return y.astype(x.dtype)


def fourier_mix(u):
    bsz, s, _ = u.shape
    ug = u.reshape(bsz, s, FNET_GROUPS, FNET_GROUP_DIM).astype(jnp.float32)
    f = jnp.fft.fftn(ug, axes=(1, 3), norm="ortho").real
    return f.reshape(bsz, s, FNET_WIDTH).astype(u.dtype)


def pool_mix(u, w_grp, b_grp, scale):
    bsz, s, _ = u.shape
    ug = u.reshape(bsz, s, POOL_GROUPS, POOL_GROUP_DIM).astype(jnp.float32)
    csum = jnp.pad(lax.cumsum(ug, axis=1), ((0, 0), (1, 0), (0, 0), (0, 0)))
    t = jnp.arange(s)
    outs = []
    for gi, w in enumerate(POOL_WINDOWS):
        lo = jnp.clip(t - w // 2, 0, s)
        hi = jnp.clip(t - w // 2 + w, 0, s)
        c = csum[:, :, gi]
        win_sum = jnp.take(c, hi, axis=1) - jnp.take(c, lo, axis=1)
        cnt = (hi - lo).astype(jnp.float32)[None, :, None]
        outs.append(win_sum / cnt - ug[:, :, gi])
    pooled = jnp.stack(outs, axis=2)
    mixed = jnp.einsum('bsgc,gcd->bsgd', pooled, w_grp.astype(jnp.float32)) + b_grp.astype(jnp.float32)
    return (mixed.reshape(bsz, s, POOL_WIDTH) * scale.astype(jnp.float32)).astype(u.dtype)


def conv_module(glu_in, w_dw, b_dw, g, b):
    a, gate = jnp.split(glu_in, 2, axis=-1)
    v = a * jax.nn.sigmoid(gate)
    v = lax.conv_general_dilated(
        v, w_dw[:, None, :].astype(v.dtype), window_strides=(1,),
        padding=((CONV_KERNEL // 2, CONV_KERNEL // 2),),
        dimension_numbers=('NWC', 'WIO', 'NWC'),
        feature_group_count=CONV_WIDTH) + b_dw
    v = layer_norm(v, g, b)
    return jax.nn.silu(v)


def encoder_layer(x, p_i, w_in, b_in, w_fnet_proj, w_pool_group, b_pool_group, pool_scale,
                  w_pool_proj, w_dw, b_dw, conv_ln_g, conv_ln_b, w_conv_proj, w_out, b_out,
                  ln_g, ln_b, w_ple, w_ple_gate, b_ple_gate):
    h = jnp.einsum('bsd,dn->bsn', x, w_in) + b_in
    fv, fz, pv, pz, cglu, cz, mg = jnp.split(h, IN_OFFSETS, axis=-1)
    ya = jnp.einsum('bsc,cd->bsd', fourier_mix(fv) * jax.nn.silu(fz), w_fnet_proj)
    yb = jnp.einsum('bsc,cd->bsd', pool_mix(pv, w_pool_group, b_pool_group, pool_scale) * jax.nn.silu(pz), w_pool_proj)
    yc = jnp.einsum('bsc,cd->bsd', conv_module(cglu, w_dw, b_dw, conv_ln_g, conv_ln_b) * jax.nn.silu(cz), w_conv_proj)
    ga, gb, gc = jnp.split(jax.nn.sigmoid(mg), N_BRANCHES, axis=-1)
    merged = ga * ya + gb * yb + gc * yc
    out = jnp.einsum('bsd,de->bse', merged, w_out) + b_out
    x = layer_norm(DEEPNORM_ALPHA * x + out, ln_g, ln_b)
    ple = jnp.einsum('bsp,pd->bsd', p_i, w_ple)
    gate = jax.nn.sigmoid(jnp.einsum('bsd,de->bse', x, w_ple_gate) + b_ple_gate)
    return x + gate * ple


def trunk(x, p, emb_ln_g, emb_ln_b, w_in, b_in, w_fnet_proj, w_pool_group, b_pool_group,
          pool_scale, w_pool_proj, w_dw, b_dw, conv_ln_g, conv_ln_b, w_conv_proj, w_out, b_out,
          ln_g, ln_b, w_ple, w_ple_gate, b_ple_gate):
    x = layer_norm(x, emb_ln_g, emb_ln_b)
    for i in range(DEPTH):
        x = encoder_layer(x, p[i], w_in[i], b_in[i], w_fnet_proj[i], w_pool_group[i], b_pool_group[i],
                          pool_scale[i], w_pool_proj[i], w_dw[i], b_dw[i], conv_ln_g[i], conv_ln_b[i],
                          w_conv_proj[i], w_out[i], b_out[i], ln_g[i], ln_b[i], w_ple[i],
                          w_ple_gate[i], b_ple_gate[i])
    return x


def setup_inputs(seed: int = 0) -> dict:
    key = jax.random.key(seed)
    ks = jax.random.split(key, 26)
    f32 = jnp.float32
    nrm = lambda k, shape, s: (jax.random.normal(k, shape, f32) * s).astype(f32)
    return {
        "x_prompt": nrm(ks[0], (BATCH, SEQ, D_MODEL), 1.0),
        "x_sample": nrm(ks[1], (DEC_BATCH, DEC_SEQ, D_MODEL), 1.0),
        "p_prompt": nrm(ks[2], (DEPTH, BATCH, SEQ, PLE_DIM), 1.0),
        "p_sample": nrm(ks[3], (DEPTH, DEC_BATCH, DEC_SEQ, PLE_DIM), 1.0),
        "emb_ln_g": 1.0 + nrm(ks[4], (D_MODEL,), 0.02),
        "emb_ln_b": nrm(ks[5], (D_MODEL,), 0.02),
        "w_in": nrm(ks[6], (DEPTH, D_MODEL, N_IN), D_MODEL ** -0.5),
        "b_in": nrm(ks[7], (DEPTH, N_IN), 0.02),
        "w_fnet_proj": nrm(ks[8], (DEPTH, FNET_WIDTH, D_MODEL), DEEPNORM_BETA * FNET_WIDTH ** -0.5),
        "w_pool_group": nrm(ks[9], (DEPTH, POOL_GROUPS, POOL_GROUP_DIM, POOL_GROUP_DIM), POOL_GROUP_DIM ** -0.5),
        "b_pool_group": nrm(ks[10], (DEPTH, POOL_GROUPS, POOL_GROUP_DIM), 0.02),
        "pool_scale": 1.0 + nrm(ks[11], (DEPTH, POOL_WIDTH), 0.1),
        "w_pool_proj": nrm(ks[12], (DEPTH, POOL_WIDTH, D_MODEL), DEEPNORM_BETA * POOL_WIDTH ** -0.5),
        "w_dw": nrm(ks[13], (DEPTH, CONV_KERNEL, CONV_WIDTH), CONV_KERNEL ** -0.5),
        "b_dw": nrm(ks[14], (DEPTH, CONV_WIDTH), 0.02),
        "conv_ln_g": 1.0 + nrm(ks[15], (DEPTH, CONV_WIDTH), 0.02),
        "conv_ln_b": nrm(ks[16], (DEPTH, CONV_WIDTH), 0.02),
        "w_conv_proj": nrm(ks[17], (DEPTH, CONV_WIDTH, D_MODEL), DEEPNORM_BETA * CONV_WIDTH ** -0.5),
        "w_out": nrm(ks[18], (DEPTH, D_MODEL, D_MODEL), DEEPNORM_BETA * D_MODEL ** -0.5),
        "b_out": nrm(ks[19], (DEPTH, D_MODEL), 0.02),
        "ln_g": 1.0 + nrm(ks[20], (DEPTH, D_MODEL), 0.02),
        "ln_b": nrm(ks[21], (DEPTH, D_MODEL), 0.02),
        "w_ple": nrm(ks[22], (DEPTH, PLE_DIM, D_MODEL), 0.5 * PLE_DIM ** -0.5),
        "w_ple_gate": nrm(ks[23], (DEPTH, D_MODEL, D_MODEL), D_MODEL ** -0.5),
        "b_ple_gate": nrm(ks[24], (DEPTH, D_MODEL), 0.02),
    }


def reference(x_prompt, x_sample, p_prompt, p_sample, emb_ln_g, emb_ln_b, w_in, b_in, w_fnet_proj,
              w_pool_group, b_pool_group, pool_scale, w_pool_proj, w_dw, b_dw, conv_ln_g, conv_ln_b,
              w_conv_proj, w_out, b_out, ln_g, ln_b, w_ple, w_ple_gate, b_ple_gate):
    y_prompt = trunk(x_prompt, p_prompt, emb_ln_g, emb_ln_b, w_in, b_in, w_fnet_proj, w_pool_group,
                     b_pool_group, pool_scale, w_pool_proj, w_dw, b_dw, conv_ln_g, conv_ln_b,
                     w_conv_proj, w_out, b_out, ln_g, ln_b, w_ple, w_ple_gate, b_ple_gate)
    y_sample = trunk(x_sample, p_sample, emb_ln_g, emb_ln_b, w_in, b_in, w_fnet_proj, w_pool_group,
                     b_pool_group, pool_scale, w_pool_proj, w_dw, b_dw, conv_ln_g, conv_ln_b,
                     w_conv_proj, w_out, b_out, ln_g, ln_b, w_ple, w_ple_gate, b_ple_gate)
    return (y_prompt, y_sample)
```

```python
import functools
import math

import jax
import jax.numpy as jnp
import numpy as np
from jax import lax
from jax.experimental import pallas as pl
from jax.experimental.pallas import tpu as pltpu

F32 = jnp.float32
BF16 = jnp.bfloat16

LN_EPS = 1e-5
FNET_GROUPS = 4
POOL_WINDOWS = (2, 4, 8, 16)
CONV_KERNEL = 31
HALO = 16

VMEM_LIMIT_BYTES = 56 * 1024 * 1024


def _cparams(*sem):
    return pltpu.CompilerParams(dimension_semantics=sem, vmem_limit_bytes=VMEM_LIMIT_BYTES)


def _const_spec(shape):
    nd = len(shape)
    return pl.BlockSpec(shape, lambda *_: (0,) * nd, pipeline_mode=pl.Buffered(1))


def _sigmoid(x):
    return 1.0 / (1.0 + jnp.exp(-x))


def _layer_norm(x, g, b):
    mu = jnp.mean(x, axis=-1, keepdims=True)
    xc = x - mu
    var = jnp.mean(xc * xc, axis=-1, keepdims=True)
    return xc * lax.rsqrt(var + LN_EPS) * g + b


def _emb_ln_kernel(x_ref, g_ref, b_ref, of_ref, ob_ref):
    y = _layer_norm(x_ref[...], g_ref[...], b_ref[...])
    of_ref[...] = y
    ob_ref[...] = y.astype(BF16)


def _emb_ln(x, g, b, tm):
    t, d = x.shape
    return pl.pallas_call(
        _emb_ln_kernel,
        grid=(t // tm,),
        in_specs=[pl.BlockSpec((tm, d), lambda i: (i, 0)), _const_spec((1, d)), _const_spec((1, d))],
        out_specs=[pl.BlockSpec((tm, d), lambda i: (i, 0)), pl.BlockSpec((tm, d), lambda i: (i, 0))],
        out_shape=[jax.ShapeDtypeStruct((t, d), F32), jax.ShapeDtypeStruct((t, d), BF16)],
        compiler_params=_cparams("parallel"),
        name="emb_ln",
    )(x, g, b)


def _in_proj_kernel(x_ref, w_ref, b_ref, o_ref, *, n_sig, n_none):
    j = pl.program_id(1)
    acc = jnp.dot(x_ref[...], w_ref[...], preferred_element_type=F32) + b_ref[...]

    @pl.when(j < n_sig)
    def _():
        o_ref[...] = _sigmoid(acc).astype(BF16)

    @pl.when(jnp.logical_and(j >= n_sig, j < n_sig + n_none))
    def _():
        o_ref[...] = acc.astype(BF16)

    @pl.when(j >= n_sig + n_none)
    def _():
        o_ref[...] = (acc * _sigmoid(acc)).astype(BF16)


def _in_proj(xb, w, b, tm, tn, n_sig, n_none):
    t, d = xb.shape
    n = w.shape[1]
    return pl.pallas_call(
        functools.partial(_in_proj_kernel, n_sig=n_sig, n_none=n_none),
        grid=(t // tm, n // tn),
        in_specs=[
            pl.BlockSpec((tm, d), lambda i, j: (i, 0)),
            pl.BlockSpec((d, tn), lambda i, j: (0, j)),
            pl.BlockSpec((1, tn), lambda i, j: (0, j)),
        ],
        out_specs=pl.BlockSpec((tm, tn), lambda i, j: (i, j)),
        out_shape=jax.ShapeDtypeStruct((t, n), BF16),
        compiler_params=_cparams("parallel", "arbitrary"),
        name="in_proj",
    )(xb, w, b)


def _fnet_kernel(c_ref, s_ref, x_ref, fz_ref, wch_ref, o_ref, p_acc, q_acc, *, gd):
    k = pl.program_id(2)

    @pl.when(k == 0)
    def _():
        p_acc[...] = jnp.zeros_like(p_acc)
        q_acc[...] = jnp.zeros_like(q_acc)

    x = x_ref[...]
    p_acc[...] += jnp.dot(c_ref[...], x, preferred_element_type=F32)
    q_acc[...] += jnp.dot(s_ref[...], x, preferred_element_type=F32)

    @pl.when(k == pl.num_programs(2) - 1)
    def _():
        for g in range(FNET_GROUPS):
            sl = slice(g * gd, (g + 1) * gd)
            pq = jnp.concatenate([p_acc[:, sl], q_acc[:, sl]], axis=1).astype(BF16)
            y = jnp.dot(pq, wch_ref[...], preferred_element_type=F32)
            o_ref[:, sl] = (y * fz_ref[:, sl].astype(F32)).astype(BF16)


def _fnet_mix(h, cs, ss, wch, bsz, s, fw, fv_off, fz_off, tm, tk):
    t = h.shape[0]
    gd = fw // FNET_GROUPS
    nm, nk = s // tm, s // tk
    return pl.pallas_call(
        functools.partial(_fnet_kernel, gd=gd),
        grid=(bsz, nm, nk),
        in_specs=[
            pl.BlockSpec((tm, tk), lambda b, i, k: (i, k)),
            pl.BlockSpec((tm, tk), lambda b, i, k: (i, k)),
            pl.BlockSpec((tk, fw), lambda b, i, k: (b * nk + k, fv_off // fw)),
            pl.BlockSpec((tm, fw), lambda b, i, k: (b * nm + i, fz_off // fw)),
            _const_spec((2 * gd, gd)),
        ],
        out_specs=pl.BlockSpec((tm, fw), lambda b, i, k: (b * nm + i, 0)),
        out_shape=jax.ShapeDtypeStruct((t, fw), BF16),
        scratch_shapes=[pltpu.VMEM((tm, fw), F32), pltpu.VMEM((tm, fw), F32)],
        compiler_params=_cparams("parallel", "parallel", "arbitrary"),
        name="fnet_mix",
    )(cs, ss, h, h, wch)


def _local_mix_kernel(pv_m, pv_p, pv_n, ca_m, ca_p, ca_n, gg_m, gg_p, gg_n, pz_ref, cz_ref,
                      wg_ref, bg_ref, sc_ref, wdw_ref, bdw_ref, cg_ref, cb_ref,
                      ub_ref, uc_ref, pvs, vs, *, ts, seq):
    i = pl.program_id(0)
    pos = (i * ts) % seq
    has_prev = pos > 0
    has_next = pos + ts < seq

    pvs[0:HALO, :] = jnp.where(has_prev, pv_p[...].astype(F32), 0.0)
    pvs[HALO:HALO + ts, :] = pv_m[...].astype(F32)
    pvs[HALO + ts:, :] = jnp.where(has_next, pv_n[...].astype(F32), 0.0)
    vs[0:HALO, :] = jnp.where(has_prev, ca_p[...].astype(F32) * gg_p[...].astype(F32), 0.0)
    vs[HALO:HALO + ts, :] = ca_m[...].astype(F32) * gg_m[...].astype(F32)
    vs[HALO + ts:, :] = jnp.where(has_next, ca_n[...].astype(F32) * gg_n[...].astype(F32), 0.0)

    pgd = pvs.shape[1] // len(POOL_WINDOWS)
    tpos = pos + lax.broadcasted_iota(jnp.int32, (ts, 1), 0)
    for gi, w in enumerate(POOL_WINDOWS):
        sl = slice(gi * pgd, (gi + 1) * pgd)
        lo = tpos - w // 2
        cnt = jnp.minimum(lo + w, seq) - jnp.maximum(lo, 0)
        inv_cnt = 1.0 / cnt.astype(F32)
        win = pvs[HALO - w // 2:HALO - w // 2 + ts, sl]
        for d in range(1 - w // 2, w // 2):
            win = win + pvs[HALO + d:HALO + d + ts, sl]
        pooled = win * inv_cnt - pvs[HALO:HALO + ts, sl]
        mixed = jnp.dot(pooled.astype(BF16), wg_ref[gi], preferred_element_type=F32) + bg_ref[:, sl]
        ub_ref[:, sl] = (mixed * sc_ref[:, sl] * pz_ref[:, sl].astype(F32)).astype(BF16)

    half = CONV_KERNEL // 2
    acc = vs[HALO - half:HALO - half + ts, :] * wdw_ref[0:1, :]
    for kk in range(1, CONV_KERNEL):
        acc = acc + vs[HALO - half + kk:HALO - half + kk + ts, :] * wdw_ref[kk:kk + 1, :]
    acc = acc + bdw_ref[...]
    y = _layer_norm(acc, cg_ref[...], cb_ref[...])
    y = y * _sigmoid(y)
    uc_ref[...] = (y * cz_ref[...].astype(F32)).astype(BF16)


def _local_mix(h, wg, bg, sc, wdw, bdw, cg, cb, seq, offs, ts):
    t = h.shape[0]
    pw, cw = wg.shape[0] * wg.shape[1], wdw.shape[1]
    r = ts // HALO
    last = t // HALO - 1

    def main(width, off):
        return pl.BlockSpec((ts, width), lambda i: (i, off // width))

    def prev(width, off):
        return pl.BlockSpec((HALO, width), lambda i: (jnp.maximum(i * r - 1, 0), off // width))

    def nxt(width, off):
        return pl.BlockSpec((HALO, width), lambda i: (jnp.minimum((i + 1) * r, last), off // width))

    in_specs = []
    for width, off in ((pw, offs["pv"]), (cw, offs["ca"]), (cw, offs["gg"])):
        in_specs += [main(width, off), prev(width, off), nxt(width, off)]
    in_specs += [main(pw, offs["pz"]), main(cw, offs["cz"])]
    in_specs += [_const_spec(wg.shape), _const_spec(bg.shape), _const_spec(sc.shape), _const_spec(wdw.shape),
                 _const_spec(bdw.shape), _const_spec(cg.shape), _const_spec(cb.shape)]
    return pl.pallas_call(
        functools.partial(_local_mix_kernel, ts=ts, seq=seq),
        grid=(t // ts,),
        in_specs=in_specs,
        out_specs=[pl.BlockSpec((ts, pw), lambda i: (i, 0)), pl.BlockSpec((ts, cw), lambda i: (i, 0))],
        out_shape=[jax.ShapeDtypeStruct((t, pw), BF16), jax.ShapeDtypeStruct((t, cw), BF16)],
        scratch_shapes=[pltpu.VMEM((ts + 2 * HALO, pw), F32), pltpu.VMEM((ts + 2 * HALO, cw), F32)],
        compiler_params=_cparams("parallel"),
        name="local_mix",
    )(*([h] * 11), wg, bg, sc, wdw, bdw, cg, cb)


def _merge_kernel(ua_ref, ub_ref, uc_ref, sg_ref, x_ref, wf_ref, wp_ref, wc_ref, wo_ref, bo_ref, g_ref, b_ref,
                  of_ref, ob_ref, m_scr, *, alpha, tc):
    d = x_ref.shape[1]
    for c in range(d // tc):
        sl = slice(c * tc, (c + 1) * tc)
        ya = jnp.dot(ua_ref[...], wf_ref[:, sl], preferred_element_type=F32)
        m = sg_ref[:, c * tc:(c + 1) * tc].astype(F32) * ya
        yb = jnp.dot(ub_ref[...], wp_ref[:, sl], preferred_element_type=F32)
        m = m + sg_ref[:, d + c * tc:d + (c + 1) * tc].astype(F32) * yb
        yc = jnp.dot(uc_ref[...], wc_ref[:, sl], preferred_element_type=F32)
        m = m + sg_ref[:, 2 * d + c * tc:2 * d + (c + 1) * tc].astype(F32) * yc
        m_scr[:, sl] = m.astype(BF16)
    out = jnp.dot(m_scr[...], wo_ref[...], preferred_element_type=F32) + bo_ref[...]
    y = _layer_norm(alpha * x_ref[...] + out, g_ref[...], b_ref[...])
    of_ref[...] = y
    ob_ref[...] = y.astype(BF16)


def _merge_out(ua, ub, uc, h, x, wf, wp, wc, wo, bo, g, b, alpha, tm):
    t, d = x.shape
    row = lambda width: pl.BlockSpec((tm, width), lambda i: (i, 0))
    return pl.pallas_call(
        functools.partial(_merge_kernel, alpha=alpha, tc=512),
        grid=(t // tm,),
        in_specs=[row(ua.shape[1]), row(ub.shape[1]), row(uc.shape[1]), row(3 * d), row(d),
                  _const_spec(wf.shape), _const_spec(wp.shape), _const_spec(wc.shape), _const_spec(wo.shape),
                  _const_spec(bo.shape), _const_spec(g.shape), _const_spec(b.shape)],
        out_specs=[row(d), row(d)],
        out_shape=[jax.ShapeDtypeStruct((t, d), F32), jax.ShapeDtypeStruct((t, d), BF16)],
        scratch_shapes=[pltpu.VMEM((tm, d), BF16)],
        compiler_params=_cparams("parallel"),
        name="merge_out",
    )(ua, ub, uc, h, x, wf, wp, wc, wo, bo, g, b)


def _ple_kernel(xf_ref, xb_ref, p_ref, wpl_ref, wg_ref, bg_ref, of_ref, ob_ref):
    gate = _sigmoid(jnp.dot(xb_ref[...], wg_ref[...], preferred_element_type=F32) + bg_ref[...])
    ple = jnp.dot(p_ref[...].astype(BF16), wpl_ref[...], preferred_element_type=F32)
    y = xf_ref[...] + gate * ple
    of_ref[...] = y
    ob_ref[...] = y.astype(BF16)


def _ple(xf, xb, p, wpl, wg, bg, tm):
    t, d = xf.shape
    row = lambda width: pl.BlockSpec((tm, width), lambda i: (i, 0))
    return pl.pallas_call(
        _ple_kernel,
        grid=(t // tm,),
        in_specs=[row(d), row(d), row(p.shape[1]), _const_spec(wpl.shape), _const_spec(wg.shape),
                  _const_spec(bg.shape)],
        out_specs=[row(d), row(d)],
        out_shape=[jax.ShapeDtypeStruct((t, d), F32), jax.ShapeDtypeStruct((t, d), BF16)],
        compiler_params=_cparams("parallel"),
        name="ple",
    )(xf, xb, p, wpl, wg, bg)


def _dft_tables(s, gd):
    n = jnp.arange(s, dtype=jnp.int32)
    ang = ((n[:, None] * n[None, :]) % s).astype(F32) * (2.0 * math.pi / s)
    scale = 1.0 / math.sqrt(s)
    cs = (jnp.cos(ang) * scale).astype(BF16)
    ss = (jnp.sin(ang) * scale).astype(BF16)
    c = jnp.arange(gd, dtype=jnp.int32)
    angc = ((c[:, None] * c[None, :]) % gd).astype(F32) * (2.0 * math.pi / gd)
    cscale = 1.0 / math.sqrt(gd)
    wch = jnp.concatenate([jnp.cos(angc) * cscale, -jnp.sin(angc) * cscale], axis=0).astype(BF16)
    return cs, ss, wch


def _pick(n, pref):
    t = min(n, pref)
    while n % t:
        t //= 2
    return t


def _trunk(x, p, prm, depth, d):
    bsz, s, _ = x.shape
    t = bsz * s
    fw, pw, cw = d // 4, d // 2, d // 4
    assert s % (2 * HALO) == 0 and fw % (128 * FNET_GROUPS) == 0
    offs = {"mg": 0, "gg": 3 * d, "fv": 3 * d + cw, "pv": 3 * d + cw + fw, "ca": 3 * d + cw + fw + pw}
    offs["cz"] = offs["ca"] + cw
    offs["pz"] = offs["cz"] + cw
    offs["fz"] = offs["pz"] + pw
    tn = 512
    n_sig, n_none = (3 * d + cw) // tn, (fw + pw + cw) // tn

    cs, ss, wch = _dft_tables(s, fw // FNET_GROUPS)
    alpha = (2.0 * depth) ** 0.25

    xf, xb = _emb_ln(x.reshape(t, d), prm["emb_ln_g"], prm["emb_ln_b"], _pick(t, 512))
    for l in range(depth):
        h = _in_proj(xb, prm["w_in"][l], prm["b_in"][l], _pick(t, 2048), tn, n_sig, n_none)
        ua = _fnet_mix(h, cs, ss, wch, bsz, s, fw, offs["fv"], offs["fz"], _pick(s, 1024), _pick(s, 1024))
        ub, uc = _local_mix(h, prm["w_pool_group"][l], prm["b_pool_group"][l], prm["pool_scale"][l],
                            prm["w_dw"][l], prm["b_dw"][l], prm["conv_ln_g"][l], prm["conv_ln_b"][l],
                            s, offs, _pick(s, 512))
        xf, xb = _merge_out(ua, ub, uc, h, xf, prm["w_fnet_proj"][l], prm["w_pool_proj"][l],
                            prm["w_conv_proj"][l], prm["w_out"][l], prm["b_out"][l], prm["ln_g"][l],
                            prm["ln_b"][l], alpha, _pick(t, 256))
        xf, xb = _ple(xf, xb, p[l].reshape(t, -1), prm["w_ple"][l], prm["w_ple_gate"][l],
                      prm["b_ple_gate"][l], _pick(t, 512))
    return xf.reshape(bsz, s, d)


def kernel(x_prompt, x_sample, p_prompt, p_sample, emb_ln_g, emb_ln_b, w_in, b_in, w_fnet_proj, w_pool_group,
           b_pool_group, pool_scale, w_pool_proj, w_dw, b_dw, conv_ln_g, conv_ln_b, w_conv_proj, w_out, b_out,
           ln_g, ln_b, w_ple, w_ple_gate, b_ple_gate):
    depth, d, _ = w_in.shape
    fw, pw, cw = d // 4, d // 2, d // 4
    o = np.cumsum([0, fw, fw, pw, pw, cw, cw, cw, 3 * d])
    seg = {k: slice(o[i], o[i + 1]) for i, k in enumerate(("fv", "fz", "pv", "pz", "ca", "gg", "cz", "mg"))}
    order = ("mg", "gg", "fv", "pv", "ca", "cz", "pz", "fz")
    w_in_r = jnp.concatenate([w_in[:, :, seg[k]] for k in order], axis=-1).astype(BF16)
    b_in_r = jnp.concatenate([b_in[:, seg[k]] for k in order], axis=-1)[:, None, :]
    row = lambda a: a[:, None, :]
    prm = dict(
        emb_ln_g=emb_ln_g[None, :], emb_ln_b=emb_ln_b[None, :], w_in=w_in_r, b_in=b_in_r,
        w_fnet_proj=w_fnet_proj.astype(BF16), w_pool_group=w_pool_group.astype(BF16),
        b_pool_group=b_pool_group.reshape(depth, 1, pw), pool_scale=row(pool_scale),
        w_pool_proj=w_pool_proj.astype(BF16), w_dw=w_dw, b_dw=row(b_dw), conv_ln_g=row(conv_ln_g),
        conv_ln_b=row(conv_ln_b), w_conv_proj=w_conv_proj.astype(BF16), w_out=w_out.astype(BF16),
        b_out=row(b_out), ln_g=row(ln_g), ln_b=row(ln_b), w_ple=w_ple.astype(BF16),
        w_ple_gate=w_ple_gate.astype(BF16), b_ple_gate=row(b_ple_gate))
    y_prompt = _trunk(x_prompt, p_prompt, prm, depth, d)
    y_sample = _trunk(x_sample, p_sample, prm, depth, d)
    return (y_prompt, y_sample)
```

```python
import functools
import math

import jax
import jax.numpy as jnp
from jax import lax
from jax.experimental import pallas as pl
from jax.experimental.pallas import tpu as pltpu

F32 = jnp.float32
BF16 = jnp.bfloat16

LN_EPS = 1e-5
FNET_GROUPS = 4
POOL_WINDOWS = (2, 4, 8, 16)
CONV_KERNEL = 31
HALO = 16
LANES = 128
SUBLANES = 8
COL_CHUNK = 512
IN_ROW_CHUNK = 512
IN_COL_CHUNK = 512

VMEM_LIMIT_BYTES = 56 * 1024 * 1024


def _cparams(*sem):
    return pltpu.CompilerParams(dimension_semantics=sem, vmem_limit_bytes=VMEM_LIMIT_BYTES)


def _const_spec(shape):
    nd = len(shape)
    return pl.BlockSpec(shape, lambda *_: (0,) * nd, pipeline_mode=pl.Buffered(1))


def _sigmoid(x):
    return 0.5 * jnp.tanh(0.5 * x) + 0.5


def _layer_norm(x, g, b):
    mu = jnp.mean(x, axis=-1, keepdims=True)
    xc = x - mu
    var = jnp.mean(xc * xc, axis=-1, keepdims=True)
    return xc * lax.rsqrt(var + LN_EPS) * g + b


def _emb_ln_kernel(x_ref, g_ref, b_ref, of_ref, ob_ref):
    y = _layer_norm(x_ref[...], g_ref[...], b_ref[...])
    of_ref[...] = y
    ob_ref[...] = y.astype(BF16)


def _emb_ln(x, g, b, tm):
    t, d = x.shape
    return pl.pallas_call(
        _emb_ln_kernel,
        grid=(t // tm,),
        in_specs=[pl.BlockSpec((tm, d), lambda i: (i, 0)), _const_spec((1, d)), _const_spec((1, d))],
        out_specs=[pl.BlockSpec((tm, d), lambda i: (i, 0)), pl.BlockSpec((tm, d), lambda i: (i, 0))],
        out_shape=[jax.ShapeDtypeStruct((t, d), F32), jax.ShapeDtypeStruct((t, d), BF16)],
        compiler_params=_cparams("parallel"),
        name="emb_ln",
    )(x, g, b)


def _in_proj_kernel(x_ref, w_ref, b_ref, ka_ref, kb_ref, kc_ref, o_ref, hf_ref, hf_scr):
    tm, tn = o_ref.shape
    n_slabs = hf_ref.shape[0]
    slabs_per_chunk = IN_COL_CHUNK // LANES
    for c in range(tn // IN_COL_CHUNK):
        sl = slice(c * IN_COL_CHUNK, (c + 1) * IN_COL_CHUNK)
        for m in range(tm // IN_ROW_CHUNK):
            rows = slice(m * IN_ROW_CHUNK, (m + 1) * IN_ROW_CHUNK)
            acc = jnp.dot(x_ref[rows, :], w_ref[:, sl], preferred_element_type=F32) + b_ref[:, sl]
            u = kb_ref[:, sl] * acc + kc_ref[:, sl]
            out = ka_ref[:, sl] * acc + u * (1.0 + jnp.tanh(0.5 * acc))
            o_ref[rows, sl] = out.astype(BF16)
            for s in range(slabs_per_chunk):
                if c * slabs_per_chunk + s < n_slabs:
                    hf_scr[c * slabs_per_chunk + s, rows, :] = out[:, s * LANES:(s + 1) * LANES]

    @pl.when(pl.program_id(1) == 0)
    def _():
        hf_ref[...] = hf_scr[...]


def _in_proj(xb, w, b, ka, kb, kc, tm, tn, n_slabs):
    t, d = xb.shape
    n = w.shape[1]
    col = lambda: pl.BlockSpec((1, tn), lambda i, j: (0, j))
    return pl.pallas_call(
        _in_proj_kernel,
        grid=(t // tm, n // tn),
        in_specs=[pl.BlockSpec((tm, d), lambda i, j: (i, 0)), pl.BlockSpec((d, tn), lambda i, j: (0, j)),
                  col(), col(), col(), col()],
        out_specs=[pl.BlockSpec((tm, tn), lambda i, j: (i, j)),
                   pl.BlockSpec((n_slabs, tm, LANES), lambda i, j: (0, i, 0))],
        out_shape=[jax.ShapeDtypeStruct((t, n), BF16), jax.ShapeDtypeStruct((n_slabs, t, LANES), F32)],
        scratch_shapes=[pltpu.VMEM((n_slabs, tm, LANES), F32)],
        compiler_params=_cparams("parallel", "arbitrary"),
        name="in_proj",
    )(xb, w, b, ka, kb, kc)


def _fnet_a_kernel(x_ref, t1_ref, w2_ref, o_ref):
    ng, n1 = x_ref.shape[0], x_ref.shape[1]
    cols = [x_ref[g, :, r, :] for g in range(ng) for r in range(SUBLANES)]
    x = jnp.concatenate(cols, axis=1).astype(BF16)
    gm = jnp.dot(t1_ref[...], x, preferred_element_type=F32)
    rows = [jnp.concatenate([gm[:n1, q * LANES:(q + 1) * LANES], gm[n1:, q * LANES:(q + 1) * LANES]], axis=1)
            for q in range(len(cols))]
    lhs = jnp.concatenate(rows, axis=0).astype(BF16)
    hm = jnp.dot(lhs, w2_ref[...], preferred_element_type=F32)
    for g in range(ng):
        for r in range(SUBLANES):
            q = g * SUBLANES + r
            o_ref[g, 0, :, r, :] = hm[q * n1:(q + 1) * n1, :LANES]
            o_ref[g, 1, :, r, :] = hm[q * n1:(q + 1) * n1, LANES:]


def _fnet_a(hf4, t1, w2, bsz, n1, n2):
    ng = FNET_GROUPS
    return pl.pallas_call(
        _fnet_a_kernel,
        grid=(bsz, n2 // SUBLANES),
        in_specs=[pl.BlockSpec((ng, n1, SUBLANES, LANES), lambda b, j: (0, b, j, 0)),
                  _const_spec(t1.shape), _const_spec(w2.shape)],
        out_specs=pl.BlockSpec((ng, 2, n1, SUBLANES, LANES), lambda b, j: (0, 0, b, j, 0)),
        out_shape=jax.ShapeDtypeStruct((ng, 2, bsz * n1, n2, LANES), F32),
        compiler_params=_cparams("parallel", "parallel"),
        name="fnet_a",
    )(hf4, t1, w2)


def _fnet_b_kernel(h_ref, m_ref, o_ref):
    ng = h_ref.shape[0]
    for kk in range(SUBLANES):
        hr = jnp.concatenate([h_ref[g, 0, kk] for g in range(ng)], axis=1)
        hi = jnp.concatenate([h_ref[g, 1, kk] for g in range(ng)], axis=1)
        rhs = jnp.concatenate([hr, hi], axis=0).astype(BF16)
        y = jnp.dot(m_ref[kk], rhs, preferred_element_type=F32)
        for g in range(ng):
            o_ref[g, :, kk, :] = y[:, g * LANES:(g + 1) * LANES]


def _fnet_b(hh, mt, bsz, n1, n2):
    ng = FNET_GROUPS
    nk = n1 // SUBLANES
    return pl.pallas_call(
        _fnet_b_kernel,
        grid=(bsz, nk),
        in_specs=[pl.BlockSpec((ng, 2, SUBLANES, n2, LANES), lambda b, k: (0, 0, b * nk + k, 0, 0)),
                  pl.BlockSpec((SUBLANES, n2, 2 * n2), lambda b, k: (k, 0, 0))],
        out_specs=pl.BlockSpec((ng, n2, SUBLANES, LANES), lambda b, k: (0, b, k, 0)),
        out_shape=jax.ShapeDtypeStruct((ng, bsz * n2, n1, LANES), F32),
        compiler_params=_cparams("parallel", "parallel"),
        name="fnet_b",
    )(hh, mt)


def _local_mix_kernel(pv_m, pv_p, pv_n, ca_m, ca_p, ca_n, gg_m, gg_p, gg_n, pz_ref, cz_ref,
                      wg_ref, bg_ref, sc_ref, wdw_ref, bdw_ref, cg_ref, cb_ref,
                      ub_ref, uc_ref, pvs, vs, vsh, *, ts, seq):
    i = pl.program_id(0)
    pos = (i * ts) % seq
    has_prev = pos > 0
    has_next = pos + ts < seq

    pvs[0:HALO, :] = jnp.where(has_prev, pv_p[...].astype(F32), 0.0)
    pvs[HALO:HALO + ts, :] = pv_m[...].astype(F32)
    pvs[HALO + ts:, :] = jnp.where(has_next, pv_n[...].astype(F32), 0.0)
    vs[0:HALO, :] = jnp.where(has_prev, ca_p[...].astype(F32) * gg_p[...].astype(F32), 0.0)
    vs[HALO:HALO + ts, :] = ca_m[...].astype(F32) * gg_m[...].astype(F32)
    vs[HALO + ts:, :] = jnp.where(has_next, ca_n[...].astype(F32) * gg_n[...].astype(F32), 0.0)

    pgd = pvs.shape[1] // len(POOL_WINDOWS)
    tpos = pos + lax.broadcasted_iota(jnp.int32, (ts, 1), 0)
    for gi, w in enumerate(POOL_WINDOWS):
        sl = slice(gi * pgd, (gi + 1) * pgd)
        lo = tpos - w // 2
        cnt = jnp.minimum(lo + w, seq) - jnp.maximum(lo, 0)
        inv_cnt = 1.0 / cnt.astype(F32)
        win = pvs[HALO - w // 2:HALO - w // 2 + ts, sl]
        for d in range(1 - w // 2, w // 2):
            win = win + pvs[HALO + d:HALO + d + ts, sl]
        pooled = win * inv_cnt - pvs[HALO:HALO + ts, sl]
        mixed = jnp.dot(pooled.astype(BF16), wg_ref[gi], preferred_element_type=F32) + bg_ref[:, sl]
        ub_ref[:, sl] = (mixed * sc_ref[:, sl] * pz_ref[:, sl].astype(F32)).astype(BF16)

    nsh = vsh.shape[1]
    for r in range(1, SUBLANES):
        vsh[r] = vs[r:r + nsh, :]
    half = CONV_KERNEL // 2
    acc = None
    for kk in range(CONV_KERNEL):
        q, r = divmod(HALO - half + kk, SUBLANES)
        rows = vs[q * SUBLANES:q * SUBLANES + ts, :] if r == 0 else vsh[r, q * SUBLANES:q * SUBLANES + ts, :]
        term = rows * wdw_ref[kk:kk + 1, :]
        acc = term if acc is None else acc + term
    acc = acc + bdw_ref[...]
    y = _layer_norm(acc, cg_ref[...], cb_ref[...])
    y = y * _sigmoid(y)
    uc_ref[...] = (y * cz_ref[...].astype(F32)).astype(BF16)


def _local_mix(h, wg, bg, sc, wdw, bdw, cg, cb, seq, offs, ts):
    t = h.shape[0]
    pw, cw = wg.shape[0] * wg.shape[1], wdw.shape[1]
    r = ts // HALO
    last = t // HALO - 1

    def main(width, off):
        return pl.BlockSpec((ts, width), lambda i: (i, off // width))

    def prev(width, off):
        return pl.BlockSpec((HALO, width), lambda i: (jnp.maximum(i * r - 1, 0), off // width))

    def nxt(width, off):
        return pl.BlockSpec((HALO, width), lambda i: (jnp.minimum((i + 1) * r, last), off // width))

    in_specs = []
    for width, off in ((pw, offs["pv"]), (cw, offs["ca"]), (cw, offs["gg"])):
        in_specs += [main(width, off), prev(width, off), nxt(width, off)]
    in_specs += [main(pw, offs["pz"]), main(cw, offs["cz"])]
    in_specs += [_const_spec(wg.shape), _const_spec(bg.shape), _const_spec(sc.shape), _const_spec(wdw.shape),
                 _const_spec(bdw.shape), _const_spec(cg.shape), _const_spec(cb.shape)]
    sh_rows = ts + ((2 * HALO - 1) // SUBLANES) * SUBLANES
    return pl.pallas_call(
        functools.partial(_local_mix_kernel, ts=ts, seq=seq),
        grid=(t // ts,),
        in_specs=in_specs,
        out_specs=[pl.BlockSpec((ts, pw), lambda i: (i, 0)), pl.BlockSpec((ts, cw), lambda i: (i, 0))],
        out_shape=[jax.ShapeDtypeStruct((t, pw), BF16), jax.ShapeDtypeStruct((t, cw), BF16)],
        scratch_shapes=[pltpu.VMEM((ts + 2 * HALO, pw), F32), pltpu.VMEM((ts + 2 * HALO, cw), F32),
                        pltpu.VMEM((SUBLANES, sh_rows, cw), F32)],
        compiler_params=_cparams("parallel"),
        name="local_mix",
    )(*([h] * 11), wg, bg, sc, wdw, bdw, cg, cb)


def _merge_kernel(ya_ref, fz_ref, ub_ref, uc_ref, g0_ref, g1_ref, g2_ref, g3_ref, x_ref, wf_ref, wp_ref, wc_ref, wo_ref,
                  bo_ref, g_ref, b_ref, of_ref, ob_ref, m_scr, *, alpha):
    d = x_ref.shape[1]
    gate_refs = (g0_ref, g1_ref, g2_ref, g3_ref)
    per_ref = g0_ref.shape[1] // COL_CHUNK
    nch = d // COL_CHUNK

    def gate(q):
        return gate_refs[q // per_ref][:, (q % per_ref) * COL_CHUNK:(q % per_ref + 1) * COL_CHUNK].astype(F32)

    ya = jnp.concatenate([ya_ref[g] for g in range(ya_ref.shape[0])], axis=1)
    ua = (ya * fz_ref[...].astype(F32)).astype(BF16)
    for c in range(nch):
        sl = slice(c * COL_CHUNK, (c + 1) * COL_CHUNK)
        m = gate(c) * jnp.dot(ua, wf_ref[:, sl], preferred_element_type=F32)
        m = m + gate(nch + c) * jnp.dot(ub_ref[...], wp_ref[:, sl], preferred_element_type=F32)
        m = m + gate(2 * nch + c) * jnp.dot(uc_ref[...], wc_ref[:, sl], preferred_element_type=F32)
        m_scr[:, sl] = m.astype(BF16)
    out = jnp.dot(m_scr[...], wo_ref[...], preferred_element_type=F32) + bo_ref[...]
    y = _layer_norm(alpha * x_ref[...] + out, g_ref[...], b_ref[...])
    of_ref[...] = y
    ob_ref[...] = y.astype(BF16)


def _merge_out(ya, ub, uc, h, x, wf, wp, wc, wo, bo, g, b, alpha, fz_off, mg_off, tm):
    t, d = x.shape
    fw = ya.shape[0] * LANES
    gw = 3 * d // 4
    assert mg_off % gw == 0 and gw % COL_CHUNK == 0
    row = lambda width: pl.BlockSpec((tm, width), lambda i: (i, 0))
    gate = lambda q: pl.BlockSpec((tm, gw), lambda i: (i, mg_off // gw + q))
    return pl.pallas_call(
        functools.partial(_merge_kernel, alpha=alpha),
        grid=(t // tm,),
        in_specs=[pl.BlockSpec((ya.shape[0], tm, LANES), lambda i: (0, i, 0)),
                  pl.BlockSpec((tm, fw), lambda i: (i, fz_off // fw)), row(ub.shape[1]), row(uc.shape[1]),
                  gate(0), gate(1), gate(2), gate(3), row(d),
                  _const_spec(wf.shape), _const_spec(wp.shape), _const_spec(wc.shape), _const_spec(wo.shape),
                  _const_spec(bo.shape), _const_spec(g.shape), _const_spec(b.shape)],
        out_specs=[row(d), row(d)],
        out_shape=[jax.ShapeDtypeStruct((t, d), F32), jax.ShapeDtypeStruct((t, d), BF16)],
        scratch_shapes=[pltpu.VMEM((tm, d), BF16)],
        compiler_params=_cparams("parallel"),
        name="merge_out",
    )(ya, h, ub, uc, h, h, h, h, x, wf, wp, wc, wo, bo, g, b)


def _ple_kernel(xf_ref, xb_ref, p_ref, wpl_ref, wg_ref, bg_ref, of_ref, ob_ref):
    gate = _sigmoid(jnp.dot(xb_ref[...], wg_ref[...], preferred_element_type=F32) + bg_ref[...])
    ple = jnp.dot(p_ref[...].astype(BF16), wpl_ref[...], preferred_element_type=F32)
    y = xf_ref[...] + gate * ple
    of_ref[...] = y
    ob_ref[...] = y.astype(BF16)


def _ple(xf, xb, p, wpl, wg, bg, tm):
    t, d = xf.shape
    row = lambda width: pl.BlockSpec((tm, width), lambda i: (i, 0))
    return pl.pallas_call(
        _ple_kernel,
        grid=(t // tm,),
        in_specs=[row(d), row(d), row(p.shape[1]), _const_spec(wpl.shape), _const_spec(wg.shape),
                  _const_spec(bg.shape)],
        out_specs=[row(d), row(d)],
        out_shape=[jax.ShapeDtypeStruct((t, d), F32), jax.ShapeDtypeStruct((t, d), BF16)],
        compiler_params=_cparams("parallel"),
        name="ple",
    )(xf, xb, p, wpl, wg, bg)


def _angles(rows, cols, period):
    prod = (rows[:, None] * cols[None, :]) % period
    return prod.astype(F32) * (2.0 * math.pi / period)


def _dft_tables(s, n1, n2, gd):
    i1 = jnp.arange(n1, dtype=jnp.int32)
    a1 = _angles(i1, i1, n1)
    t1 = (jnp.concatenate([jnp.cos(a1), -jnp.sin(a1)], axis=0) / math.sqrt(n1)).astype(BF16)
    ic = jnp.arange(gd, dtype=jnp.int32)
    ac = _angles(ic, ic, gd)
    cc, sc = jnp.cos(ac) / math.sqrt(gd), jnp.sin(ac) / math.sqrt(gd)
    w2 = jnp.concatenate([jnp.concatenate([cc, -sc], axis=1), jnp.concatenate([sc, cc], axis=1)], axis=0)
    k = jnp.arange(s, dtype=jnp.int32)
    am = _angles(k, jnp.arange(n2, dtype=jnp.int32), s)
    m = jnp.concatenate([jnp.cos(am), jnp.sin(am)], axis=1) / math.sqrt(n2)
    mt = m.reshape(n2, n1, 2 * n2).transpose(1, 0, 2).astype(BF16)
    return t1, w2.astype(BF16), mt


def _pick(n, pref):
    t = min(n, pref)
    while n % t:
        t //= 2
    return t


def _trunk(x, p, prm, depth, d):
    bsz, s, _ = x.shape
    t = bsz * s
    fw, pw, cw = d // 4, d // 2, d // 4
    gd = fw // FNET_GROUPS
    assert gd == LANES and s % (2 * HALO) == 0
    offs = {"fv": 0, "fz": fw, "pv": 2 * fw, "pz": 2 * fw + pw, "ca": 2 * fw + 2 * pw}
    offs["gg"] = offs["ca"] + cw
    offs["cz"] = offs["gg"] + cw
    offs["mg"] = offs["cz"] + cw
    n_in = offs["mg"] + 3 * d
    tn = 3 * COL_CHUNK
    assert n_in % tn == 0 and 2 * fw <= tn
    n_slabs = FNET_GROUPS

    n2 = 1 << ((s.bit_length() - 1 + 1) // 2)
    n1 = s // n2
    assert n1 * n2 == s and n1 % SUBLANES == 0 and n2 % SUBLANES == 0
    t1, w2, mt = _dft_tables(s, n1, n2, gd)
    alpha = (2.0 * depth) ** 0.25

    xf, xb = _emb_ln(x.reshape(t, d), prm["emb_ln_g"], prm["emb_ln_b"], _pick(t, 512))
    for l in range(depth):
        h, hf = _in_proj(xb, prm["w_in"][l], prm["b_in"][l], prm["ka"], prm["kb"], prm["kc"],
                         _pick(t, 2048), tn, n_slabs)
        hh = _fnet_a(hf.reshape(n_slabs, bsz * n1, n2, LANES), t1, w2, bsz, n1, n2)
        ua = _fnet_b(hh, mt, bsz, n1, n2)
        ua = ua.reshape(FNET_GROUPS, t, LANES)
        ub, uc = _local_mix(h, prm["w_pool_group"][l], prm["b_pool_group"][l], prm["pool_scale"][l],
                            prm["w_dw"][l], prm["b_dw"][l], prm["conv_ln_g"][l], prm["conv_ln_b"][l],
                            s, offs, _pick(s, 512))
        xf, xb = _merge_out(ua, ub, uc, h, xf, prm["w_fnet_proj"][l], prm["w_pool_proj"][l],
                            prm["w_conv_proj"][l], prm["w_out"][l], prm["b_out"][l], prm["ln_g"][l],
                            prm["ln_b"][l], alpha, offs["fz"], offs["mg"], _pick(t, 256))
        xf, xb = _ple(xf, xb, p[l].reshape(t, -1), prm["w_ple"][l], prm["w_ple_gate"][l],
                      prm["b_ple_gate"][l], _pick(t, 512))
    return xf.reshape(bsz, s, d)


def kernel(x_prompt, x_sample, p_prompt, p_sample, emb_ln_g, emb_ln_b, w_in, b_in, w_fnet_proj, w_pool_group,
           b_pool_group, pool_scale, w_pool_proj, w_dw, b_dw, conv_ln_g, conv_ln_b, w_conv_proj, w_out, b_out,
           ln_g, ln_b, w_ple, w_ple_gate, b_ple_gate):
    depth, d, n_in = w_in.shape
    fw, pw, cw = d // 4, d // 2, d // 4
    ident, silu, sigm = (1.0, 0.0, 0.0), (0.0, 0.5, 0.0), (0.0, 0.0, 0.5)
    kinds = [(fw, ident), (fw, silu), (pw, ident), (pw, silu), (cw, ident), (cw, sigm), (cw, silu), (3 * d, sigm)]
    ka, kb, kc = (jnp.concatenate([jnp.full((1, n), k[j], F32) for n, k in kinds], axis=1) for j in range(3))
    row = lambda a: a[:, None, :]
    prm = dict(
        emb_ln_g=emb_ln_g[None, :], emb_ln_b=emb_ln_b[None, :], w_in=w_in.astype(BF16), b_in=row(b_in),
        ka=ka, kb=kb, kc=kc,
        w_fnet_proj=w_fnet_proj.astype(BF16), w_pool_group=w_pool_group.astype(BF16),
        b_pool_group=b_pool_group.reshape(depth, 1, pw), pool_scale=row(pool_scale),
        w_pool_proj=w_pool_proj.astype(BF16), w_dw=w_dw, b_dw=row(b_dw), conv_ln_g=row(conv_ln_g),
        conv_ln_b=row(conv_ln_b), w_conv_proj=w_conv_proj.astype(BF16), w_out=w_out.astype(BF16),
        b_out=row(b_out), ln_g=row(ln_g), ln_b=row(ln_b), w_ple=w_ple.astype(BF16),
        w_ple_gate=w_ple_gate.astype(BF16), b_ple_gate=row(b_ple_gate))
    y_prompt = _trunk(x_prompt, p_prompt, prm, depth, d)
    y_sample = _trunk(x_sample, p_sample, prm, depth, d)
    return (y_prompt, y_sample)
```

```python
import functools
import math

import jax
import jax.numpy as jnp
from jax import lax
from jax.experimental import pallas as pl
from jax.experimental.pallas import tpu as pltpu

F32 = jnp.float32
BF16 = jnp.bfloat16

LN_EPS = 1e-5
FNET_GROUPS = 4
POOL_WINDOWS = (2, 4, 8, 16)
CONV_KERNEL = 31
HALO = 16
LANES = 128
SUBLANES = 8
COL_CHUNK = 512
IN_ROW_CHUNK = 512
IN_COL_CHUNK = 512

VMEM_LIMIT_BYTES = 56 * 1024 * 1024


def _cparams(*sem):
    return pltpu.CompilerParams(dimension_semantics=sem, vmem_limit_bytes=VMEM_LIMIT_BYTES)


def _const_spec(shape):
    nd = len(shape)
    return pl.BlockSpec(shape, lambda *_: (0,) * nd, pipeline_mode=pl.Buffered(1))


def _layer_spec(arr, l):
    tail = tuple(arr.shape[1:])
    return pl.BlockSpec((None,) + tail, lambda *_: (l,) + (0,) * len(tail), pipeline_mode=pl.Buffered(1))


def _sigmoid(x):
    return 0.5 * jnp.tanh(0.5 * x) + 0.5


def _layer_norm(x, g, b):
    mu = jnp.mean(x, axis=-1, keepdims=True)
    xc = x - mu
    var = jnp.mean(xc * xc, axis=-1, keepdims=True)
    return xc * lax.rsqrt(var + LN_EPS) * g + b


def _emb_ln_kernel(x_ref, g_ref, b_ref, of_ref, ob_ref):
    y = _layer_norm(x_ref[...], g_ref[...], b_ref[...])
    of_ref[...] = y
    ob_ref[...] = y.astype(BF16)


def _emb_ln(x, g, b, tm):
    t, d = x.shape
    return pl.pallas_call(
        _emb_ln_kernel,
        grid=(t // tm,),
        in_specs=[pl.BlockSpec((tm, d), lambda i: (i, 0)), _const_spec((1, d)), _const_spec((1, d))],
        out_specs=[pl.BlockSpec((tm, d), lambda i: (i, 0)), pl.BlockSpec((tm, d), lambda i: (i, 0))],
        out_shape=[jax.ShapeDtypeStruct((t, d), F32), jax.ShapeDtypeStruct((t, d), BF16)],
        compiler_params=_cparams("parallel"),
        name="emb_ln",
    )(x, g, b)


def _in_proj_kernel(x_ref, w_ref, b_ref, o_ref, hf_ref, *, step_groups):
    tm = o_ref.shape[0]
    n_slabs = hf_ref.shape[0]
    slabs_per_piece = IN_COL_CHUNK // LANES

    def body(kinds, keep_f32):
        for c, kind in enumerate(kinds):
            sl = slice(c * IN_COL_CHUNK, (c + 1) * IN_COL_CHUNK)
            for m in range(tm // IN_ROW_CHUNK):
                rows = slice(m * IN_ROW_CHUNK, (m + 1) * IN_ROW_CHUNK)
                acc = jnp.dot(x_ref[rows, :], w_ref[:, sl], preferred_element_type=F32) + b_ref[:, sl]
                if kind == "silu":
                    acc = acc * _sigmoid(acc)
                elif kind == "sigmoid":
                    acc = _sigmoid(acc)
                o_ref[rows, sl] = acc.astype(BF16)
                if keep_f32:
                    for s in range(slabs_per_piece):
                        if c * slabs_per_piece + s < n_slabs:
                            hf_ref[c * slabs_per_piece + s, rows, :] = acc[:, s * LANES:(s + 1) * LANES]

    j = pl.program_id(1)
    for kinds, lo, hi in step_groups:
        pl.when(jnp.logical_and(j >= lo, j < hi))(functools.partial(body, kinds, lo == 0))


def _in_proj(xb, w, b, l, tm, tn, n_slabs, step_groups):
    t, d = xb.shape
    n = w.shape[2]
    return pl.pallas_call(
        functools.partial(_in_proj_kernel, step_groups=step_groups),
        grid=(t // tm, n // tn),
        in_specs=[pl.BlockSpec((tm, d), lambda i, j: (i, 0)), pl.BlockSpec((None, d, tn), lambda i, j: (l, 0, j)),
                  pl.BlockSpec((None, 1, tn), lambda i, j: (l, 0, j))],
        out_specs=[pl.BlockSpec((tm, tn), lambda i, j: (i, j)),
                   pl.BlockSpec((n_slabs, tm, LANES), lambda i, j: (0, i, 0))],
        out_shape=[jax.ShapeDtypeStruct((t, n), BF16), jax.ShapeDtypeStruct((n_slabs, t, LANES), F32)],
        compiler_params=_cparams("parallel", "arbitrary"),
        name="in_proj",
    )(xb, w, b)


def _fnet_a_kernel(x_ref, t1_ref, w2_ref, o_ref):
    ng, n1 = x_ref.shape[0], x_ref.shape[1]
    cols = [x_ref[g, :, r, :] for g in range(ng) for r in range(SUBLANES)]
    x = jnp.concatenate(cols, axis=1).astype(BF16)
    gm = jnp.dot(t1_ref[...], x, preferred_element_type=F32)
    rows = [jnp.concatenate([gm[:n1, q * LANES:(q + 1) * LANES], gm[n1:, q * LANES:(q + 1) * LANES]], axis=1)
            for q in range(len(cols))]
    lhs = jnp.concatenate(rows, axis=0).astype(BF16)
    hm = jnp.dot(lhs, w2_ref[...], preferred_element_type=F32)
    for g in range(ng):
        for r in range(SUBLANES):
            q = g * SUBLANES + r
            o_ref[g, 0, :, r, :] = hm[q * n1:(q + 1) * n1, :LANES]
            o_ref[g, 1, :, r, :] = hm[q * n1:(q + 1) * n1, LANES:]


def _fnet_a(hf4, t1, w2, bsz, n1, n2):
    ng = FNET_GROUPS
    return pl.pallas_call(
        _fnet_a_kernel,
        grid=(bsz, n2 // SUBLANES),
        in_specs=[pl.BlockSpec((ng, n1, SUBLANES, LANES), lambda b, j: (0, b, j, 0)),
                  _const_spec(t1.shape), _const_spec(w2.shape)],
        out_specs=pl.BlockSpec((ng, 2, n1, SUBLANES, LANES), lambda b, j: (0, 0, b, j, 0)),
        out_shape=jax.ShapeDtypeStruct((ng, 2, bsz * n1, n2, LANES), F32),
        compiler_params=_cparams("parallel", "parallel"),
        name="fnet_a",
    )(hf4, t1, w2)


def _fnet_b_kernel(h_ref, m_ref, o_ref):
    ng = h_ref.shape[0]
    for kk in range(SUBLANES):
        hr = jnp.concatenate([h_ref[g, 0, kk] for g in range(ng)], axis=1)
        hi = jnp.concatenate([h_ref[g, 1, kk] for g in range(ng)], axis=1)
        rhs = jnp.concatenate([hr, hi], axis=0).astype(BF16)
        y = jnp.dot(m_ref[kk], rhs, preferred_element_type=F32)
        for g in range(ng):
            o_ref[g, :, kk, :] = y[:, g * LANES:(g + 1) * LANES]


def _fnet_b(hh, mt, bsz, n1, n2):
    ng = FNET_GROUPS
    nk = n1 // SUBLANES
    return pl.pallas_call(
        _fnet_b_kernel,
        grid=(bsz, nk),
        in_specs=[pl.BlockSpec((ng, 2, SUBLANES, n2, LANES), lambda b, k: (0, 0, b * nk + k, 0, 0)),
                  pl.BlockSpec((SUBLANES, n2, 2 * n2), lambda b, k: (k, 0, 0))],
        out_specs=pl.BlockSpec((ng, n2, SUBLANES, LANES), lambda b, k: (0, b, k, 0)),
        out_shape=jax.ShapeDtypeStruct((ng, bsz * n2, n1, LANES), F32),
        compiler_params=_cparams("parallel", "parallel"),
        name="fnet_b",
    )(hh, mt)


def _local_mix_kernel(pv_m, pv_p, pv_n, ca_m, ca_p, ca_n, gg_m, gg_p, gg_n, pz_ref, cz_ref,
                      wg_ref, bg_ref, sc_ref, wdw_ref, bdw_ref, cg_ref, cb_ref,
                      ub_ref, uc_ref, pvs, vs, vsh, *, ts, seq):
    i = pl.program_id(0)
    pos = (i * ts) % seq
    has_prev = pos > 0
    has_next = pos + ts < seq

    pvs[0:HALO, :] = jnp.where(has_prev, pv_p[...].astype(F32), 0.0)
    pvs[HALO:HALO + ts, :] = pv_m[...].astype(F32)
    pvs[HALO + ts:, :] = jnp.where(has_next, pv_n[...].astype(F32), 0.0)
    vs[0:HALO, :] = jnp.where(has_prev, ca_p[...].astype(F32) * gg_p[...].astype(F32), 0.0)
    vs[HALO:HALO + ts, :] = ca_m[...].astype(F32) * gg_m[...].astype(F32)
    vs[HALO + ts:, :] = jnp.where(has_next, ca_n[...].astype(F32) * gg_n[...].astype(F32), 0.0)

    pgd = pvs.shape[1] // len(POOL_WINDOWS)
    tpos = pos + lax.broadcasted_iota(jnp.int32, (ts, 1), 0)
    for gi, w in enumerate(POOL_WINDOWS):
        sl = slice(gi * pgd, (gi + 1) * pgd)
        lo = tpos - w // 2
        cnt = jnp.minimum(lo + w, seq) - jnp.maximum(lo, 0)
        inv_cnt = 1.0 / cnt.astype(F32)
        win = pvs[HALO - w // 2:HALO - w // 2 + ts, sl]
        for d in range(1 - w // 2, w // 2):
            win = win + pvs[HALO + d:HALO + d + ts, sl]
        pooled = win * inv_cnt - pvs[HALO:HALO + ts, sl]
        mixed = jnp.dot(pooled.astype(BF16), wg_ref[gi], preferred_element_type=F32) + bg_ref[:, sl]
        ub_ref[:, sl] = (mixed * sc_ref[:, sl] * pz_ref[:, sl].astype(F32)).astype(BF16)

    nsh = vsh.shape[1]
    for r in range(1, SUBLANES):
        vsh[r] = vs[r:r + nsh, :]
    half = CONV_KERNEL // 2
    acc = None
    for kk in range(CONV_KERNEL):
        q, r = divmod(HALO - half + kk, SUBLANES)
        rows = vs[q * SUBLANES:q * SUBLANES + ts, :] if r == 0 else vsh[r, q * SUBLANES:q * SUBLANES + ts, :]
        term = rows * wdw_ref[kk:kk + 1, :]
        acc = term if acc is None else acc + term
    acc = acc + bdw_ref[...]
    y = _layer_norm(acc, cg_ref[...], cb_ref[...])
    y = y * _sigmoid(y)
    uc_ref[...] = (y * cz_ref[...].astype(F32)).astype(BF16)


def _local_mix(h, wg, bg, sc, wdw, bdw, cg, cb, l, seq, offs, ts):
    t = h.shape[0]
    pw, cw = wg.shape[1] * wg.shape[2], wdw.shape[2]
    r = ts // HALO
    last = t // HALO - 1

    def main(width, off):
        return pl.BlockSpec((ts, width), lambda i: (i, off // width))

    def prev(width, off):
        return pl.BlockSpec((HALO, width), lambda i: (jnp.maximum(i * r - 1, 0), off // width))

    def nxt(width, off):
        return pl.BlockSpec((HALO, width), lambda i: (jnp.minimum((i + 1) * r, last), off // width))

    in_specs = []
    for width, off in ((pw, offs["pv"]), (cw, offs["ca"]), (cw, offs["gg"])):
        in_specs += [main(width, off), prev(width, off), nxt(width, off)]
    in_specs += [main(pw, offs["pz"]), main(cw, offs["cz"])]
    in_specs += [_layer_spec(a, l) for a in (wg, bg, sc, wdw, bdw, cg, cb)]
    sh_rows = ts + ((2 * HALO - 1) // SUBLANES) * SUBLANES
    return pl.pallas_call(
        functools.partial(_local_mix_kernel, ts=ts, seq=seq),
        grid=(t // ts,),
        in_specs=in_specs,
        out_specs=[pl.BlockSpec((ts, pw), lambda i: (i, 0)), pl.BlockSpec((ts, cw), lambda i: (i, 0))],
        out_shape=[jax.ShapeDtypeStruct((t, pw), BF16), jax.ShapeDtypeStruct((t, cw), BF16)],
        scratch_shapes=[pltpu.VMEM((ts + 2 * HALO, pw), F32), pltpu.VMEM((ts + 2 * HALO, cw), F32),
                        pltpu.VMEM((SUBLANES, sh_rows, cw), F32)],
        compiler_params=_cparams("parallel"),
        name="local_mix",
    )(*([h] * 11), wg, bg, sc, wdw, bdw, cg, cb)


def _merge_kernel(ya_ref, fz_ref, ub_ref, uc_ref, g0_ref, g1_ref, g2_ref, g3_ref, x_ref, wf_ref, wp_ref, wc_ref, wo_ref,
                  bo_ref, g_ref, b_ref, of_ref, ob_ref, m_scr, *, alpha):
    d = x_ref.shape[1]
    gate_refs = (g0_ref, g1_ref, g2_ref, g3_ref)
    per_ref = g0_ref.shape[1] // COL_CHUNK
    nch = d // COL_CHUNK

    def gate(q):
        return _sigmoid(
            gate_refs[q // per_ref][:, (q % per_ref) * COL_CHUNK:(q % per_ref + 1) * COL_CHUNK].astype(F32))

    ya = jnp.concatenate([ya_ref[g] for g in range(ya_ref.shape[0])], axis=1)
    ua = (ya * fz_ref[...].astype(F32)).astype(BF16)
    for c in range(nch):
        sl = slice(c * COL_CHUNK, (c + 1) * COL_CHUNK)
        m = gate(c) * jnp.dot(ua, wf_ref[:, sl], preferred_element_type=F32)
        m = m + gate(nch + c) * jnp.dot(ub_ref[...], wp_ref[:, sl], preferred_element_type=F32)
        m = m + gate(2 * nch + c) * jnp.dot(uc_ref[...], wc_ref[:, sl], preferred_element_type=F32)
        m_scr[:, sl] = m.astype(BF16)
    out = jnp.dot(m_scr[...], wo_ref[...], preferred_element_type=F32) + bo_ref[...]
    y = _layer_norm(alpha * x_ref[...] + out, g_ref[...], b_ref[...])
    of_ref[...] = y
    ob_ref[...] = y.astype(BF16)


def _merge_out(ya, ub, uc, h, x, wf, wp, wc, wo, bo, g, b, l, alpha, fz_off, mg_off, tm):
    t, d = x.shape
    fw = ya.shape[0] * LANES
    gw = 3 * d // 4
    assert mg_off % gw == 0 and gw % COL_CHUNK == 0
    row = lambda width: pl.BlockSpec((tm, width), lambda i: (i, 0))
    gate = lambda q: pl.BlockSpec((tm, gw), lambda i: (i, mg_off // gw + q))
    return pl.pallas_call(
        functools.partial(_merge_kernel, alpha=alpha),
        grid=(t // tm,),
        in_specs=[pl.BlockSpec((ya.shape[0], tm, LANES), lambda i: (0, i, 0)),
                  pl.BlockSpec((tm, fw), lambda i: (i, fz_off // fw)), row(ub.shape[1]), row(uc.shape[1]),
                  gate(0), gate(1), gate(2), gate(3), row(d)
                  ] + [_layer_spec(a, l) for a in (wf, wp, wc, wo, bo, g, b)],
        out_specs=[row(d), row(d)],
        out_shape=[jax.ShapeDtypeStruct((t, d), F32), jax.ShapeDtypeStruct((t, d), BF16)],
        scratch_shapes=[pltpu.VMEM((tm, d), BF16)],
        compiler_params=_cparams("parallel"),
        name="merge_out",
    )(ya, h, ub, uc, h, h, h, h, x, wf, wp, wc, wo, bo, g, b)


def _ple_kernel(xf_ref, xb_ref, p_ref, wpl_ref, wg_ref, bg_ref, of_ref, ob_ref):
    gate = _sigmoid(jnp.dot(xb_ref[...], wg_ref[...], preferred_element_type=F32) + bg_ref[...])
    ple = jnp.dot(p_ref[...].astype(BF16), wpl_ref[...], preferred_element_type=F32)
    y = xf_ref[...] + gate * ple
    of_ref[...] = y
    ob_ref[...] = y.astype(BF16)


def _ple(xf, xb, p, wpl, wg, bg, l, tm):
    t, d = xf.shape
    row = lambda width: pl.BlockSpec((tm, width), lambda i: (i, 0))
    return pl.pallas_call(
        _ple_kernel,
        grid=(t // tm,),
        in_specs=[row(d), row(d), pl.BlockSpec((None, tm, p.shape[2]), lambda i: (l, i, 0)),
                  _layer_spec(wpl, l), _layer_spec(wg, l), _layer_spec(bg, l)],
        out_specs=[row(d), row(d)],
        out_shape=[jax.ShapeDtypeStruct((t, d), F32), jax.ShapeDtypeStruct((t, d), BF16)],
        compiler_params=_cparams("parallel"),
        name="ple",
    )(xf, xb, p, wpl, wg, bg)


def _angles(rows, cols, period):
    prod = (rows[:, None] * cols[None, :]) % period
    return prod.astype(F32) * (2.0 * math.pi / period)


def _dft_tables(s, n1, n2, gd):
    i1 = jnp.arange(n1, dtype=jnp.int32)
    a1 = _angles(i1, i1, n1)
    t1 = (jnp.concatenate([jnp.cos(a1), -jnp.sin(a1)], axis=0) / math.sqrt(n1)).astype(BF16)
    ic = jnp.arange(gd, dtype=jnp.int32)
    ac = _angles(ic, ic, gd)
    cc, sc = jnp.cos(ac) / math.sqrt(gd), jnp.sin(ac) / math.sqrt(gd)
    w2 = jnp.concatenate([jnp.concatenate([cc, -sc], axis=1), jnp.concatenate([sc, cc], axis=1)], axis=0)
    k = jnp.arange(s, dtype=jnp.int32)
    am = _angles(k, jnp.arange(n2, dtype=jnp.int32), s)
    m = jnp.concatenate([jnp.cos(am), jnp.sin(am)], axis=1) / math.sqrt(n2)
    mt = m.reshape(n2, n1, 2 * n2).transpose(1, 0, 2).astype(BF16)
    return t1, w2.astype(BF16), mt


def _pick(n, pref):
    t = min(n, pref)
    while n % t:
        t //= 2
    return t


def _trunk(x, p, prm, depth, d):
    bsz, s, _ = x.shape
    t = bsz * s
    fw, pw, cw = d // 4, d // 2, d // 4
    gd = fw // FNET_GROUPS
    assert gd == LANES and s % (2 * HALO) == 0
    segs = (("fv", fw, "none"), ("fz", fw, "silu"), ("pv", pw, "none"), ("pz", pw, "silu"),
            ("ca", cw, "none"), ("gg", cw, "sigmoid"), ("cz", cw, "silu"), ("mg", 3 * d, "none"))
    offs, piece_kinds, off = {}, [], 0
    for name, width, kind in segs:
        assert width % IN_COL_CHUNK == 0
        offs[name] = off
        off += width
        piece_kinds += [kind] * (width // IN_COL_CHUNK)
    tn = 3 * IN_COL_CHUNK
    assert off % tn == 0 and fw <= tn
    per_step = tn // IN_COL_CHUNK
    steps = [tuple(piece_kinds[i:i + per_step]) for i in range(0, len(piece_kinds), per_step)]
    groups = [[steps[0], 0, 1]]
    for j in range(1, len(steps)):
        if j > 1 and steps[j] == groups[-1][0]:
            groups[-1][2] = j + 1
        else:
            groups.append([steps[j], j, j + 1])
    step_groups = tuple(tuple(g) for g in groups)
    n_slabs = FNET_GROUPS

    n2 = 1 << ((s.bit_length() - 1 + 1) // 2)
    n1 = s // n2
    assert n1 * n2 == s and n1 % SUBLANES == 0 and n2 % SUBLANES == 0
    t1, w2, mt = _dft_tables(s, n1, n2, gd)
    alpha = (2.0 * depth) ** 0.25
    p = p.reshape(depth, t, -1)

    xf, xb = _emb_ln(x.reshape(t, d), prm["emb_ln_g"], prm["emb_ln_b"], _pick(t, 512))
    for l in range(depth):
        h, hf = _in_proj(xb, prm["w_in"], prm["b_in"], l, _pick(t, 2048), tn, n_slabs, step_groups)
        hh = _fnet_a(hf.reshape(n_slabs, bsz * n1, n2, LANES), t1, w2, bsz, n1, n2)
        ya = _fnet_b(hh, mt, bsz, n1, n2).reshape(FNET_GROUPS, t, LANES)
        ub, uc = _local_mix(h, prm["w_pool_group"], prm["b_pool_group"], prm["pool_scale"], prm["w_dw"],
                            prm["b_dw"], prm["conv_ln_g"], prm["conv_ln_b"], l, s, offs, _pick(s, 512))
        xf, xb = _merge_out(ya, ub, uc, h, xf, prm["w_fnet_proj"], prm["w_pool_proj"], prm["w_conv_proj"],
                            prm["w_out"], prm["b_out"], prm["ln_g"], prm["ln_b"], l, alpha, offs["fz"],
                            offs["mg"], _pick(t, 256))
        xf, xb = _ple(xf, xb, p, prm["w_ple"], prm["w_ple_gate"], prm["b_ple_gate"], l, _pick(t, 512))
    return xf.reshape(bsz, s, d)


def kernel(x_prompt, x_sample, p_prompt, p_sample, emb_ln_g, emb_ln_b, w_in, b_in, w_fnet_proj, w_pool_group,
           b_pool_group, pool_scale, w_pool_proj, w_dw, b_dw, conv_ln_g, conv_ln_b, w_conv_proj, w_out, b_out,
           ln_g, ln_b, w_ple, w_ple_gate, b_ple_gate):
    depth, d, _ = w_in.shape
    pw = d // 2
    row = lambda a: a[:, None, :]
    prm = dict(
        emb_ln_g=emb_ln_g[None, :], emb_ln_b=emb_ln_b[None, :], w_in=w_in.astype(BF16), b_in=row(b_in),
        w_fnet_proj=w_fnet_proj.astype(BF16), w_pool_group=w_pool_group.astype(BF16),
        b_pool_group=b_pool_group.reshape(depth, 1, pw), pool_scale=row(pool_scale),
        w_pool_proj=w_pool_proj.astype(BF16), w_dw=w_dw, b_dw=row(b_dw), conv_ln_g=row(conv_ln_g),
        conv_ln_b=row(conv_ln_b), w_conv_proj=w_conv_proj.astype(BF16), w_out=w_out.astype(BF16),
        b_out=row(b_out), ln_g=row(ln_g), ln_b=row(ln_b), w_ple=w_ple.astype(BF16),
        w_ple_gate=w_ple_gate.astype(BF16), b_ple_gate=row(b_ple_gate))
    y_prompt = _trunk(x_prompt, p_prompt, prm, depth, d)
    y_sample = _trunk(x_sample, p_sample, prm, depth, d)
    return (y_prompt, y_sample)
```

```python
import functools
import math

import jax
import jax.numpy as jnp
from jax import lax
from jax.experimental import pallas as pl
from jax.experimental.pallas import tpu as pltpu

F32 = jnp.float32
BF16 = jnp.bfloat16

LN_EPS = 1e-5
FNET_GROUPS = 4
POOL_WINDOWS = (2, 4, 8, 16)
CONV_KERNEL = 31
HALO = 16
LANES = 128
SUBLANES = 8
COL_CHUNK = 512
IN_ROW_CHUNK = 512
IN_COL_CHUNK = 512

VMEM_LIMIT_BYTES = 56 * 1024 * 1024


def _cparams(*sem):
    return pltpu.CompilerParams(dimension_semantics=sem, vmem_limit_bytes=VMEM_LIMIT_BYTES)


def _const_spec(shape):
    nd = len(shape)
    return pl.BlockSpec(shape, lambda *_: (0,) * nd, pipeline_mode=pl.Buffered(1))


def _layer_spec(arr, l):
    tail = tuple(arr.shape[1:])
    return pl.BlockSpec((None,) + tail, lambda *_: (l,) + (0,) * len(tail), pipeline_mode=pl.Buffered(1))


def _sigmoid(x):
    return 0.5 * jnp.tanh(0.5 * x) + 0.5


def _layer_norm(x, g, b):
    mu = jnp.mean(x, axis=-1, keepdims=True)
    xc = x - mu
    var = jnp.mean(xc * xc, axis=-1, keepdims=True)
    return xc * lax.rsqrt(var + LN_EPS) * g + b


def _emb_ln_kernel(x_ref, g_ref, b_ref, of_ref, ob_ref):
    y = _layer_norm(x_ref[...], g_ref[...], b_ref[...])
    of_ref[...] = y
    ob_ref[...] = y.astype(BF16)


def _emb_ln(x, g, b, tm):
    t, d = x.shape
    return pl.pallas_call(
        _emb_ln_kernel,
        grid=(t // tm,),
        in_specs=[pl.BlockSpec((tm, d), lambda i: (i, 0)), _const_spec((1, d)), _const_spec((1, d))],
        out_specs=[pl.BlockSpec((tm, d), lambda i: (i, 0)), pl.BlockSpec((tm, d), lambda i: (i, 0))],
        out_shape=[jax.ShapeDtypeStruct((t, d), F32), jax.ShapeDtypeStruct((t, d), BF16)],
        compiler_params=_cparams("parallel"),
        name="emb_ln",
    )(x, g, b)


def _in_proj_kernel(x_ref, w_ref, b_ref, o_ref, hf_ref, *, step_groups):
    tm = o_ref.shape[0]
    n_slabs = hf_ref.shape[0]
    slabs_per_piece = IN_COL_CHUNK // LANES

    def body(kinds, keep_f32):
        for c, kind in enumerate(kinds):
            sl = slice(c * IN_COL_CHUNK, (c + 1) * IN_COL_CHUNK)
            for m in range(tm // IN_ROW_CHUNK):
                rows = slice(m * IN_ROW_CHUNK, (m + 1) * IN_ROW_CHUNK)
                acc = jnp.dot(x_ref[rows, :], w_ref[:, sl], preferred_element_type=F32) + b_ref[:, sl]
                if kind == "silu":
                    acc = acc * _sigmoid(acc)
                elif kind == "sigmoid":
                    acc = _sigmoid(acc)
                o_ref[rows, sl] = acc.astype(BF16)
                if keep_f32:
                    for s in range(slabs_per_piece):
                        if c * slabs_per_piece + s < n_slabs:
                            hf_ref[c * slabs_per_piece + s, rows, :] = acc[:, s * LANES:(s + 1) * LANES]

    j = pl.program_id(1)
    for kinds, lo, hi in step_groups:
        pl.when(jnp.logical_and(j >= lo, j < hi))(functools.partial(body, kinds, lo == 0))


def _in_proj(xb, w, b, l, tm, tn, n_slabs, step_groups):
    t, d = xb.shape
    n = w.shape[2]
    return pl.pallas_call(
        functools.partial(_in_proj_kernel, step_groups=step_groups),
        grid=(t // tm, n // tn),
        in_specs=[pl.BlockSpec((tm, d), lambda i, j: (i, 0)), pl.BlockSpec((None, d, tn), lambda i, j: (l, 0, j)),
                  pl.BlockSpec((None, 1, tn), lambda i, j: (l, 0, j))],
        out_specs=[pl.BlockSpec((tm, tn), lambda i, j: (i, j)),
                   pl.BlockSpec((n_slabs, tm, LANES), lambda i, j: (0, i, 0))],
        out_shape=[jax.ShapeDtypeStruct((t, n), BF16), jax.ShapeDtypeStruct((n_slabs, t, LANES), F32)],
        compiler_params=_cparams("parallel", "arbitrary"),
        name="in_proj",
    )(xb, w, b)


def _fnet_a_kernel(x_ref, t1_ref, w2_ref, o_ref):
    ng, n1 = x_ref.shape[0], x_ref.shape[1]
    rows = n1 * SUBLANES
    x = jnp.concatenate([x_ref[g].reshape(rows, LANES) for g in range(ng)], axis=1).astype(BF16)
    gm = jnp.dot(t1_ref[...], x, preferred_element_type=F32)
    for g in range(ng):
        ln = slice(g * LANES, (g + 1) * LANES)
        lhs = jnp.concatenate([gm[:rows, ln], gm[rows:, ln]], axis=1).astype(BF16)
        hm = jnp.dot(lhs, w2_ref[...], preferred_element_type=F32)
        o_ref[g, 0] = hm[:, :LANES].reshape(n1, SUBLANES, LANES)
        o_ref[g, 1] = hm[:, LANES:].reshape(n1, SUBLANES, LANES)


def _fnet_a(hf4, t1, w2, bsz, n1, n2):
    ng = FNET_GROUPS
    return pl.pallas_call(
        _fnet_a_kernel,
        grid=(bsz, n2 // SUBLANES),
        in_specs=[pl.BlockSpec((ng, n1, SUBLANES, LANES), lambda b, j: (0, b, j, 0)),
                  _const_spec(t1.shape), _const_spec(w2.shape)],
        out_specs=pl.BlockSpec((ng, 2, n1, SUBLANES, LANES), lambda b, j: (0, 0, b, j, 0)),
        out_shape=jax.ShapeDtypeStruct((ng, 2, bsz * n1, n2, LANES), F32),
        compiler_params=_cparams("parallel", "parallel"),
        name="fnet_a",
    )(hf4, t1, w2)


def _fnet_b_kernel(h_ref, m_ref, o_ref):
    ng = h_ref.shape[0]
    for kk in range(SUBLANES):
        hr = jnp.concatenate([h_ref[g, 0, kk] for g in range(ng)], axis=1)
        hi = jnp.concatenate([h_ref[g, 1, kk] for g in range(ng)], axis=1)
        rhs = jnp.concatenate([hr, hi], axis=0).astype(BF16)
        y = jnp.dot(m_ref[kk], rhs, preferred_element_type=F32)
        for g in range(ng):
            o_ref[g, :, kk, :] = y[:, g * LANES:(g + 1) * LANES]


def _fnet_b(hh, mt, bsz, n1, n2):
    ng = FNET_GROUPS
    nk = n1 // SUBLANES
    return pl.pallas_call(
        _fnet_b_kernel,
        grid=(bsz, nk),
        in_specs=[pl.BlockSpec((ng, 2, SUBLANES, n2, LANES), lambda b, k: (0, 0, b * nk + k, 0, 0)),
                  pl.BlockSpec((SUBLANES, n2, 2 * n2), lambda b, k: (k, 0, 0))],
        out_specs=pl.BlockSpec((ng, n2, SUBLANES, LANES), lambda b, k: (0, b, k, 0)),
        out_shape=jax.ShapeDtypeStruct((ng, bsz * n2, n1, LANES), F32),
        compiler_params=_cparams("parallel", "parallel"),
        name="fnet_b",
    )(hh, mt)


def _local_mix_kernel(pv_m, pv_p, pv_n, ca_m, ca_p, ca_n, gg_m, gg_p, gg_n, pz_ref, cz_ref,
                      wg_ref, bg_ref, sc_ref, wdw_ref, bdw_ref, cg_ref, cb_ref,
                      ub_ref, uc_ref, pvs, vs, vsh, *, ts, seq):
    i = pl.program_id(0)
    pos = (i * ts) % seq
    has_prev = pos > 0
    has_next = pos + ts < seq

    pvs[0:HALO, :] = jnp.where(has_prev, pv_p[...].astype(F32), 0.0)
    pvs[HALO:HALO + ts, :] = pv_m[...].astype(F32)
    pvs[HALO + ts:, :] = jnp.where(has_next, pv_n[...].astype(F32), 0.0)
    vs[0:HALO, :] = jnp.where(has_prev, ca_p[...].astype(F32) * gg_p[...].astype(F32), 0.0)
    vs[HALO:HALO + ts, :] = ca_m[...].astype(F32) * gg_m[...].astype(F32)
    vs[HALO + ts:, :] = jnp.where(has_next, ca_n[...].astype(F32) * gg_n[...].astype(F32), 0.0)

    pgd = pvs.shape[1] // len(POOL_WINDOWS)
    tpos = pos + lax.broadcasted_iota(jnp.int32, (ts, 1), 0)
    for gi, w in enumerate(POOL_WINDOWS):
        sl = slice(gi * pgd, (gi + 1) * pgd)
        lo = tpos - w // 2
        cnt = jnp.minimum(lo + w, seq) - jnp.maximum(lo, 0)
        inv_cnt = 1.0 / cnt.astype(F32)
        win = pvs[HALO - w // 2:HALO - w // 2 + ts, sl]
        for d in range(1 - w // 2, w // 2):
            win = win + pvs[HALO + d:HALO + d + ts, sl]
        pooled = win * inv_cnt - pvs[HALO:HALO + ts, sl]
        mixed = jnp.dot(pooled.astype(BF16), wg_ref[gi], preferred_element_type=F32) + bg_ref[:, sl]
        ub_ref[:, sl] = (mixed * sc_ref[:, sl] * pz_ref[:, sl].astype(F32)).astype(BF16)

    nsh = vsh.shape[1]
    for r in range(1, SUBLANES):
        vsh[r] = vs[r:r + nsh, :]
    half = CONV_KERNEL // 2
    acc = None
    for kk in range(CONV_KERNEL):
        q, r = divmod(HALO - half + kk, SUBLANES)
        rows = vs[q * SUBLANES:q * SUBLANES + ts, :] if r == 0 else vsh[r, q * SUBLANES:q * SUBLANES + ts, :]
        term = rows * wdw_ref[kk:kk + 1, :]
        acc = term if acc is None else acc + term
    acc = acc + bdw_ref[...]
    y = _layer_norm(acc, cg_ref[...], cb_ref[...])
    y = y * _sigmoid(y)
    uc_ref[...] = (y * cz_ref[...].astype(F32)).astype(BF16)


def _local_mix(h, wg, bg, sc, wdw, bdw, cg, cb, l, seq, offs, ts):
    t = h.shape[0]
    pw, cw = wg.shape[1] * wg.shape[2], wdw.shape[2]
    r = ts // HALO
    last = t // HALO - 1

    def main(width, off):
        return pl.BlockSpec((ts, width), lambda i: (i, off // width))

    def prev(width, off):
        return pl.BlockSpec((HALO, width), lambda i: (jnp.maximum(i * r - 1, 0), off // width))

    def nxt(width, off):
        return pl.BlockSpec((HALO, width), lambda i: (jnp.minimum((i + 1) * r, last), off // width))

    in_specs = []
    for width, off in ((pw, offs["pv"]), (cw, offs["ca"]), (cw, offs["gg"])):
        in_specs += [main(width, off), prev(width, off), nxt(width, off)]
    in_specs += [main(pw, offs["pz"]), main(cw, offs["cz"])]
    in_specs += [_layer_spec(a, l) for a in (wg, bg, sc, wdw, bdw, cg, cb)]
    sh_rows = ts + ((2 * HALO - 1) // SUBLANES) * SUBLANES
    return pl.pallas_call(
        functools.partial(_local_mix_kernel, ts=ts, seq=seq),
        grid=(t // ts,),
        in_specs=in_specs,
        out_specs=[pl.BlockSpec((ts, pw), lambda i: (i, 0)), pl.BlockSpec((ts, cw), lambda i: (i, 0))],
        out_shape=[jax.ShapeDtypeStruct((t, pw), BF16), jax.ShapeDtypeStruct((t, cw), BF16)],
        scratch_shapes=[pltpu.VMEM((ts + 2 * HALO, pw), F32), pltpu.VMEM((ts + 2 * HALO, cw), F32),
                        pltpu.VMEM((SUBLANES, sh_rows, cw), F32)],
        compiler_params=_cparams("parallel"),
        name="local_mix",
    )(*([h] * 11), wg, bg, sc, wdw, bdw, cg, cb)


def _merge_proj_kernel(ya_ref, fz_ref, ub_ref, uc_ref, g0_ref, g1_ref, g2_ref, g3_ref, wf_ref, wp_ref, wc_ref, m_ref):
    d = m_ref.shape[1]
    gate_refs = (g0_ref, g1_ref, g2_ref, g3_ref)
    per_ref = g0_ref.shape[1] // COL_CHUNK
    nch = d // COL_CHUNK

    def gate(q):
        return _sigmoid(
            gate_refs[q // per_ref][:, (q % per_ref) * COL_CHUNK:(q % per_ref + 1) * COL_CHUNK].astype(F32))

    ya = jnp.concatenate([ya_ref[g] for g in range(ya_ref.shape[0])], axis=1)
    ua = (ya * fz_ref[...].astype(F32)).astype(BF16)
    for c in range(nch):
        sl = slice(c * COL_CHUNK, (c + 1) * COL_CHUNK)
        m = gate(c) * jnp.dot(ua, wf_ref[:, sl], preferred_element_type=F32)
        m = m + gate(nch + c) * jnp.dot(ub_ref[...], wp_ref[:, sl], preferred_element_type=F32)
        m = m + gate(2 * nch + c) * jnp.dot(uc_ref[...], wc_ref[:, sl], preferred_element_type=F32)
        m_ref[:, sl] = m.astype(BF16)


def _merge_proj(ya, ub, uc, h, wf, wp, wc, l, fz_off, mg_off, tm):
    t = h.shape[0]
    d = wf.shape[2]
    fw = ya.shape[0] * LANES
    gw = 3 * d // 4
    assert mg_off % gw == 0 and gw % COL_CHUNK == 0
    row = lambda width: pl.BlockSpec((tm, width), lambda i: (i, 0))
    gate = lambda q: pl.BlockSpec((tm, gw), lambda i: (i, mg_off // gw + q))
    return pl.pallas_call(
        _merge_proj_kernel,
        grid=(t // tm,),
        in_specs=[pl.BlockSpec((ya.shape[0], tm, LANES), lambda i: (0, i, 0)),
                  pl.BlockSpec((tm, fw), lambda i: (i, fz_off // fw)), row(ub.shape[1]), row(uc.shape[1]),
                  gate(0), gate(1), gate(2), gate(3)] + [_layer_spec(a, l) for a in (wf, wp, wc)],
        out_specs=row(d),
        out_shape=jax.ShapeDtypeStruct((t, d), BF16),
        compiler_params=_cparams("parallel"),
        name="merge_proj",
    )(ya, h, ub, uc, h, h, h, h, wf, wp, wc)


def _out_ple_kernel(m_ref, x_ref, p_ref, wo_ref, bo_ref, g_ref, b_ref, wpl_ref, wg_ref, bg_ref, of_ref, ob_ref,
                    z_scr, xb_scr, *, alpha):
    d = x_ref.shape[1]
    nch = d // COL_CHUNK
    for c in range(nch):
        sl = slice(c * COL_CHUNK, (c + 1) * COL_CHUNK)
        out = jnp.dot(m_ref[...], wo_ref[:, sl], preferred_element_type=F32) + bo_ref[:, sl]
        z_scr[:, sl] = alpha * x_ref[:, sl] + out
    xn = _layer_norm(z_scr[...], g_ref[...], b_ref[...])
    z_scr[...] = xn
    xb_scr[...] = xn.astype(BF16)
    pb = p_ref[...].astype(BF16)
    for c in range(nch):
        sl = slice(c * COL_CHUNK, (c + 1) * COL_CHUNK)
        gate = _sigmoid(jnp.dot(xb_scr[...], wg_ref[:, sl], preferred_element_type=F32) + bg_ref[:, sl])
        ple = jnp.dot(pb, wpl_ref[:, sl], preferred_element_type=F32)
        y = z_scr[:, sl] + gate * ple
        of_ref[:, sl] = y
        ob_ref[:, sl] = y.astype(BF16)


def _out_ple(m, x, p, wo, bo, g, b, wpl, wg, bg, l, alpha, tm):
    t, d = x.shape
    row = lambda width: pl.BlockSpec((tm, width), lambda i: (i, 0))
    return pl.pallas_call(
        functools.partial(_out_ple_kernel, alpha=alpha),
        grid=(t // tm,),
        in_specs=[row(d), row(d), pl.BlockSpec((None, tm, p.shape[2]), lambda i: (l, i, 0))]
        + [_layer_spec(a, l) for a in (wo, bo, g, b, wpl, wg, bg)],
        out_specs=[row(d), row(d)],
        out_shape=[jax.ShapeDtypeStruct((t, d), F32), jax.ShapeDtypeStruct((t, d), BF16)],
        scratch_shapes=[pltpu.VMEM((tm, d), F32), pltpu.VMEM((tm, d), BF16)],
        compiler_params=_cparams("parallel"),
        name="out_ple",
    )(m, x, p, wo, bo, g, b, wpl, wg, bg)


def _angles(rows, cols, period):
    prod = (rows[:, None] * cols[None, :]) % period
    return prod.astype(F32) * (2.0 * math.pi / period)


def _dft_tables(s, n1, n2, gd):
    i1 = jnp.arange(n1, dtype=jnp.int32)
    a1 = _angles(i1, i1, n1)
    t1 = jnp.concatenate([jnp.cos(a1), -jnp.sin(a1)], axis=0) / math.sqrt(n1)
    t1 = jnp.kron(t1, jnp.eye(SUBLANES, dtype=F32)).astype(BF16)
    ic = jnp.arange(gd, dtype=jnp.int32)
    ac = _angles(ic, ic, gd)
    cc, sc = jnp.cos(ac) / math.sqrt(gd), jnp.sin(ac) / math.sqrt(gd)
    w2 = jnp.concatenate([jnp.concatenate([cc, -sc], axis=1), jnp.concatenate([sc, cc], axis=1)], axis=0)
    k = jnp.arange(s, dtype=jnp.int32)
    am = _angles(k, jnp.arange(n2, dtype=jnp.int32), s)
    m = jnp.concatenate([jnp.cos(am), jnp.sin(am)], axis=1) / math.sqrt(n2)
    mt = m.reshape(n2, n1, 2 * n2).transpose(1, 0, 2).astype(BF16)
    return t1, w2.astype(BF16), mt


def _pick(n, pref):
    t = min(n, pref)
    while n % t:
        t //= 2
    return t


def _trunk(x, p, prm, depth, d):
    bsz, s, _ = x.shape
    t = bsz * s
    fw, pw, cw = d // 4, d // 2, d // 4
    gd = fw // FNET_GROUPS
    assert gd == LANES and s % (2 * HALO) == 0
    segs = (("fv", fw, "none"), ("fz", fw, "silu"), ("pv", pw, "none"), ("pz", pw, "silu"),
            ("ca", cw, "none"), ("gg", cw, "sigmoid"), ("cz", cw, "silu"), ("mg", 3 * d, "none"))
    offs, piece_kinds, off = {}, [], 0
    for name, width, kind in segs:
        assert width % IN_COL_CHUNK == 0
        offs[name] = off
        off += width
        piece_kinds += [kind] * (width // IN_COL_CHUNK)
    tn = 3 * IN_COL_CHUNK
    assert off % tn == 0 and fw <= tn
    per_step = tn // IN_COL_CHUNK
    steps = [tuple(piece_kinds[i:i + per_step]) for i in range(0, len(piece_kinds), per_step)]
    groups = [[steps[0], 0, 1]]
    for j in range(1, len(steps)):
        if j > 1 and steps[j] == groups[-1][0]:
            groups[-1][2] = j + 1
        else:
            groups.append([steps[j], j, j + 1])
    step_groups = tuple(tuple(g) for g in groups)
    n_slabs = FNET_GROUPS

    n2 = 1 << ((s.bit_length() - 1 + 1) // 2)
    n1 = s // n2
    assert n1 * n2 == s and n1 % SUBLANES == 0 and n2 % SUBLANES == 0
    t1, w2, mt = _dft_tables(s, n1, n2, gd)
    alpha = (2.0 * depth) ** 0.25
    p = p.reshape(depth, t, -1)

    xf, xb = _emb_ln(x.reshape(t, d), prm["emb_ln_g"], prm["emb_ln_b"], _pick(t, 512))
    for l in range(depth):
        h, hf = _in_proj(xb, prm["w_in"], prm["b_in"], l, _pick(t, 2048), tn, n_slabs, step_groups)
        hh = _fnet_a(hf.reshape(n_slabs, bsz * n1, n2, LANES), t1, w2, bsz, n1, n2)
        ya = _fnet_b(hh, mt, bsz, n1, n2).reshape(FNET_GROUPS, t, LANES)
        ub, uc = _local_mix(h, prm["w_pool_group"], prm["b_pool_group"], prm["pool_scale"], prm["w_dw"],
                            prm["b_dw"], prm["conv_ln_g"], prm["conv_ln_b"], l, s, offs, _pick(s, 512))
        m = _merge_proj(ya, ub, uc, h, prm["w_fnet_proj"], prm["w_pool_proj"], prm["w_conv_proj"], l,
                        offs["fz"], offs["mg"], _pick(t, 512))
        xf, xb = _out_ple(m, xf, p, prm["w_out"], prm["b_out"], prm["ln_g"], prm["ln_b"], prm["w_ple"],
                          prm["w_ple_gate"], prm["b_ple_gate"], l, alpha, _pick(t, 512))
    return xf.reshape(bsz, s, d)


def kernel(x_prompt, x_sample, p_prompt, p_sample, emb_ln_g, emb_ln_b, w_in, b_in, w_fnet_proj, w_pool_group,
           b_pool_group, pool_scale, w_pool_proj, w_dw, b_dw, conv_ln_g, conv_ln_b, w_conv_proj, w_out, b_out,
           ln_g, ln_b, w_ple, w_ple_gate, b_ple_gate):
    depth, d, _ = w_in.shape
    pw = d // 2
    row = lambda a: a[:, None, :]
    prm = dict(
        emb_ln_g=emb_ln_g[None, :], emb_ln_b=emb_ln_b[None, :], w_in=w_in.astype(BF16), b_in=row(b_in),
        w_fnet_proj=w_fnet_proj.astype(BF16), w_pool_group=w_pool_group.astype(BF16),
        b_pool_group=b_pool_group.reshape(depth, 1, pw), pool_scale=row(pool_scale),
        w_pool_proj=w_pool_proj.astype(BF16), w_dw=w_dw, b_dw=row(b_dw), conv_ln_g=row(conv_ln_g),
        conv_ln_b=row(conv_ln_b), w_conv_proj=w_conv_proj.astype(BF16), w_out=w_out.astype(BF16),
        b_out=row(b_out), ln_g=row(ln_g), ln_b=row(ln_b), w_ple=w_ple.astype(BF16),
        w_ple_gate=w_ple_gate.astype(BF16), b_ple_gate=row(b_ple_gate))
    y_prompt = _trunk(x_prompt, p_prompt, prm, depth, d)
    y_sample = _trunk(x_sample, p_sample, prm, depth, d)
    return (y_prompt, y_sample)
```

```python
import functools
import math

import jax
import jax.numpy as jnp
from jax import lax
from jax.experimental import pallas as pl
from jax.experimental.pallas import tpu as pltpu

F32 = jnp.float32
BF16 = jnp.bfloat16

LN_EPS = 1e-5
FNET_GROUPS = 4
POOL_WINDOWS = (2, 4, 8, 16)
CONV_KERNEL = 31
HALO = 16
LANES = 128
SUBLANES = 8
COL_CHUNK = 512
IN_ROW_CHUNK = 512
IN_COL_CHUNK = 512

VMEM_LIMIT_BYTES = 56 * 1024 * 1024


def _cparams(*sem):
    return pltpu.CompilerParams(dimension_semantics=sem, vmem_limit_bytes=VMEM_LIMIT_BYTES)


def _const_spec(shape):
    nd = len(shape)
    return pl.BlockSpec(shape, lambda *_: (0,) * nd, pipeline_mode=pl.Buffered(1))


def _layer_spec(arr, l):
    tail = tuple(arr.shape[1:])
    return pl.BlockSpec((None,) + tail, lambda *_: (l,) + (0,) * len(tail), pipeline_mode=pl.Buffered(1))


def _sigmoid(x):
    return 0.5 * jnp.tanh(0.5 * x) + 0.5


def _layer_norm(x, g, b):
    mu = jnp.mean(x, axis=-1, keepdims=True)
    xc = x - mu
    var = jnp.mean(xc * xc, axis=-1, keepdims=True)
    return xc * lax.rsqrt(var + LN_EPS) * g + b


def _emb_ln_kernel(x_ref, g_ref, b_ref, of_ref, ob_ref):
    y = _layer_norm(x_ref[...], g_ref[...], b_ref[...])
    of_ref[...] = y
    ob_ref[...] = y.astype(BF16)


def _emb_ln(x, g, b, tm):
    t, d = x.shape
    return pl.pallas_call(
        _emb_ln_kernel,
        grid=(t // tm,),
        in_specs=[pl.BlockSpec((tm, d), lambda i: (i, 0)), _const_spec((1, d)), _const_spec((1, d))],
        out_specs=[pl.BlockSpec((tm, d), lambda i: (i, 0)), pl.BlockSpec((tm, d), lambda i: (i, 0))],
        out_shape=[jax.ShapeDtypeStruct((t, d), F32), jax.ShapeDtypeStruct((t, d), BF16)],
        compiler_params=_cparams("parallel"),
        name="emb_ln",
    )(x, g, b)


def _in_proj_kernel(x_ref, w_ref, b_ref, o_ref, hf_ref, *, step_groups):
    tm = o_ref.shape[0]
    n_slabs = hf_ref.shape[0]
    slabs_per_piece = IN_COL_CHUNK // LANES

    def body(kinds, keep_f32):
        for c, kind in enumerate(kinds):
            sl = slice(c * IN_COL_CHUNK, (c + 1) * IN_COL_CHUNK)
            for m in range(tm // IN_ROW_CHUNK):
                rows = slice(m * IN_ROW_CHUNK, (m + 1) * IN_ROW_CHUNK)
                acc = jnp.dot(x_ref[rows, :], w_ref[:, sl], preferred_element_type=F32) + b_ref[:, sl]
                if kind == "silu":
                    acc = acc * _sigmoid(acc)
                elif kind == "sigmoid":
                    acc = _sigmoid(acc)
                o_ref[rows, sl] = acc.astype(BF16)
                if keep_f32:
                    for s in range(slabs_per_piece):
                        if c * slabs_per_piece + s < n_slabs:
                            hf_ref[c * slabs_per_piece + s, rows, :] = acc[:, s * LANES:(s + 1) * LANES]

    j = pl.program_id(1)
    for kinds, lo, hi in step_groups:
        pl.when(jnp.logical_and(j >= lo, j < hi))(functools.partial(body, kinds, lo == 0))


def _in_proj(xb, w, b, l, tm, tn, n_slabs, step_groups):
    t, d = xb.shape
    n = w.shape[2]
    return pl.pallas_call(
        functools.partial(_in_proj_kernel, step_groups=step_groups),
        grid=(t // tm, n // tn),
        in_specs=[pl.BlockSpec((tm, d), lambda i, j: (i, 0)), pl.BlockSpec((None, d, tn), lambda i, j: (l, 0, j)),
                  pl.BlockSpec((None, 1, tn), lambda i, j: (l, 0, j))],
        out_specs=[pl.BlockSpec((tm, tn), lambda i, j: (i, j)),
                   pl.BlockSpec((n_slabs, tm, LANES), lambda i, j: (0, i, 0))],
        out_shape=[jax.ShapeDtypeStruct((t, n), BF16), jax.ShapeDtypeStruct((n_slabs, t, LANES), F32)],
        compiler_params=_cparams("parallel", "arbitrary"),
        name="in_proj",
    )(xb, w, b)


def _fnet_a_kernel(x_ref, t1_ref, w2_ref, o_ref):
    ng, n1 = x_ref.shape[0], x_ref.shape[1]
    rows = n1 * SUBLANES
    x = jnp.concatenate([x_ref[g].reshape(rows, LANES) for g in range(ng)], axis=1).astype(BF16)
    gm = jnp.dot(t1_ref[...], x, preferred_element_type=F32)
    for g in range(ng):
        ln = slice(g * LANES, (g + 1) * LANES)
        lhs = jnp.concatenate([gm[:rows, ln], gm[rows:, ln]], axis=1).astype(BF16)
        hm = jnp.dot(lhs, w2_ref[...], preferred_element_type=F32)
        o_ref[g, 0] = hm[:, :LANES].reshape(n1, SUBLANES, LANES)
        o_ref[g, 1] = hm[:, LANES:].reshape(n1, SUBLANES, LANES)


def _fnet_a(hf4, t1, w2, bsz, n1, n2):
    ng = FNET_GROUPS
    return pl.pallas_call(
        _fnet_a_kernel,
        grid=(bsz, n2 // SUBLANES),
        in_specs=[pl.BlockSpec((ng, n1, SUBLANES, LANES), lambda b, j: (0, b, j, 0)),
                  _const_spec(t1.shape), _const_spec(w2.shape)],
        out_specs=pl.BlockSpec((ng, 2, n1, SUBLANES, LANES), lambda b, j: (0, 0, b, j, 0)),
        out_shape=jax.ShapeDtypeStruct((ng, 2, bsz * n1, n2, LANES), F32),
        compiler_params=_cparams("parallel", "parallel"),
        name="fnet_a",
    )(hf4, t1, w2)


def _fnet_b_kernel(h_ref, m_ref, o_ref):
    ng = h_ref.shape[0]
    for kk in range(SUBLANES):
        hr = jnp.concatenate([h_ref[g, 0, kk] for g in range(ng)], axis=1)
        hi = jnp.concatenate([h_ref[g, 1, kk] for g in range(ng)], axis=1)
        rhs = jnp.concatenate([hr, hi], axis=0).astype(BF16)
        y = jnp.dot(m_ref[kk], rhs, preferred_element_type=F32)
        for g in range(ng):
            o_ref[g, :, kk, :] = y[:, g * LANES:(g + 1) * LANES]


def _fnet_b(hh, mt, bsz, n1, n2):
    ng = FNET_GROUPS
    nk = n1 // SUBLANES
    return pl.pallas_call(
        _fnet_b_kernel,
        grid=(bsz, nk),
        in_specs=[pl.BlockSpec((ng, 2, SUBLANES, n2, LANES), lambda b, k: (0, 0, b * nk + k, 0, 0)),
                  pl.BlockSpec((SUBLANES, n2, 2 * n2), lambda b, k: (k, 0, 0))],
        out_specs=pl.BlockSpec((ng, n2, SUBLANES, LANES), lambda b, k: (0, b, k, 0)),
        out_shape=jax.ShapeDtypeStruct((ng, bsz * n2, n1, LANES), F32),
        compiler_params=_cparams("parallel", "parallel"),
        name="fnet_b",
    )(hh, mt)


def _local_mix_kernel(pv_m, pv_p, pv_n, ca_m, ca_p, ca_n, gg_m, gg_p, gg_n, pz_ref, cz_ref,
                      wg_ref, bg_ref, sc_ref, wdw_ref, bdw_ref, cg_ref, cb_ref,
                      ub_ref, uc_ref, pvs, vs, vsh, psum, *, ts, seq):
    i = pl.program_id(0)
    pos = (i * ts) % seq
    has_prev = pos > 0
    has_next = pos + ts < seq

    pvs[0:HALO, :] = jnp.where(has_prev, pv_p[...].astype(F32), 0.0)
    pvs[HALO:HALO + ts, :] = pv_m[...].astype(F32)
    pvs[HALO + ts:, :] = jnp.where(has_next, pv_n[...].astype(F32), 0.0)
    vs[0:HALO, :] = jnp.where(has_prev, ca_p[...].astype(F32) * gg_p[...].astype(F32), 0.0)
    vs[HALO:HALO + ts, :] = ca_m[...].astype(F32) * gg_m[...].astype(F32)
    vs[HALO + ts:, :] = jnp.where(has_next, ca_n[...].astype(F32) * gg_n[...].astype(F32), 0.0)

    pgd = pvs.shape[1] // len(POOL_WINDOWS)
    nrows = pvs.shape[0]
    psum[0:SUBLANES, :] = jnp.zeros((SUBLANES, pgd), F32)
    tpos = pos + lax.broadcasted_iota(jnp.int32, (ts, 1), 0)
    for gi, w in enumerate(POOL_WINDOWS):
        sl = slice(gi * pgd, (gi + 1) * pgd)
        lo = tpos - w // 2
        cnt = jnp.minimum(lo + w, seq) - jnp.maximum(lo, 0)
        inv_cnt = 1.0 / cnt.astype(F32)
        cur = pvs[SUBLANES:nrows, sl] + pvs[SUBLANES - 1:nrows - 1, sl]
        span = 2
        while span < w:
            psum[SUBLANES:nrows, :] = cur
            cur = cur + psum[SUBLANES - span:nrows - span, :]
            span *= 2
        psum[SUBLANES:nrows, :] = cur
        win = psum[HALO + w // 2 - 1:HALO + w // 2 - 1 + ts, :]
        pooled = win * inv_cnt - pvs[HALO:HALO + ts, sl]
        mixed = jnp.dot(pooled.astype(BF16), wg_ref[gi], preferred_element_type=F32) + bg_ref[:, sl]
        ub_ref[:, sl] = (mixed * sc_ref[:, sl] * pz_ref[:, sl].astype(F32)).astype(BF16)

    nsh = vsh.shape[1]
    for r in range(1, SUBLANES):
        vsh[r] = vs[r:r + nsh, :]
    half = CONV_KERNEL // 2
    acc = None
    for kk in range(CONV_KERNEL):
        q, r = divmod(HALO - half + kk, SUBLANES)
        rows = vs[q * SUBLANES:q * SUBLANES + ts, :] if r == 0 else vsh[r, q * SUBLANES:q * SUBLANES + ts, :]
        term = rows * wdw_ref[kk:kk + 1, :]
        acc = term if acc is None else acc + term
    acc = acc + bdw_ref[...]
    y = _layer_norm(acc, cg_ref[...], cb_ref[...])
    y = y * _sigmoid(y)
    uc_ref[...] = (y * cz_ref[...].astype(F32)).astype(BF16)


def _local_mix(h, wg, bg, sc, wdw, bdw, cg, cb, l, seq, offs, ts):
    t = h.shape[0]
    pw, cw = wg.shape[1] * wg.shape[2], wdw.shape[2]
    r = ts // HALO
    last = t // HALO - 1

    def main(width, off):
        return pl.BlockSpec((ts, width), lambda i: (i, off // width))

    def prev(width, off):
        return pl.BlockSpec((HALO, width), lambda i: (jnp.maximum(i * r - 1, 0), off // width))

    def nxt(width, off):
        return pl.BlockSpec((HALO, width), lambda i: (jnp.minimum((i + 1) * r, last), off // width))

    in_specs = []
    for width, off in ((pw, offs["pv"]), (cw, offs["ca"]), (cw, offs["gg"])):
        in_specs += [main(width, off), prev(width, off), nxt(width, off)]
    in_specs += [main(pw, offs["pz"]), main(cw, offs["cz"])]
    in_specs += [_layer_spec(a, l) for a in (wg, bg, sc, wdw, bdw, cg, cb)]
    sh_rows = ts + ((2 * HALO - 1) // SUBLANES) * SUBLANES
    return pl.pallas_call(
        functools.partial(_local_mix_kernel, ts=ts, seq=seq),
        grid=(t // ts,),
        in_specs=in_specs,
        out_specs=[pl.BlockSpec((ts, pw), lambda i: (i, 0)), pl.BlockSpec((ts, cw), lambda i: (i, 0))],
        out_shape=[jax.ShapeDtypeStruct((t, pw), BF16), jax.ShapeDtypeStruct((t, cw), BF16)],
        scratch_shapes=[pltpu.VMEM((ts + 2 * HALO, pw), F32), pltpu.VMEM((ts + 2 * HALO, cw), F32),
                        pltpu.VMEM((SUBLANES, sh_rows, cw), F32),
                        pltpu.VMEM((ts + 2 * HALO, pw // len(POOL_WINDOWS)), F32)],
        compiler_params=_cparams("parallel"),
        name="local_mix",
    )(*([h] * 11), wg, bg, sc, wdw, bdw, cg, cb)


def _merge_proj_kernel(ya_ref, fz_ref, ub_ref, uc_ref, g0_ref, g1_ref, g2_ref, g3_ref, wf_ref, wp_ref, wc_ref, m_ref):
    d = m_ref.shape[1]
    gate_refs = (g0_ref, g1_ref, g2_ref, g3_ref)
    per_ref = g0_ref.shape[1] // COL_CHUNK
    nch = d // COL_CHUNK

    def gate(q):
        return _sigmoid(
            gate_refs[q // per_ref][:, (q % per_ref) * COL_CHUNK:(q % per_ref + 1) * COL_CHUNK].astype(F32))

    ya = jnp.concatenate([ya_ref[g] for g in range(ya_ref.shape[0])], axis=1)
    ua = (ya * fz_ref[...].astype(F32)).astype(BF16)
    for c in range(nch):
        sl = slice(c * COL_CHUNK, (c + 1) * COL_CHUNK)
        m = gate(c) * jnp.dot(ua, wf_ref[:, sl], preferred_element_type=F32)
        m = m + gate(nch + c) * jnp.dot(ub_ref[...], wp_ref[:, sl], preferred_element_type=F32)
        m = m + gate(2 * nch + c) * jnp.dot(uc_ref[...], wc_ref[:, sl], preferred_element_type=F32)
        m_ref[:, sl] = m.astype(BF16)


def _merge_proj(ya, ub, uc, h, wf, wp, wc, l, fz_off, mg_off, tm):
    t = h.shape[0]
    d = wf.shape[2]
    fw = ya.shape[0] * LANES
    gw = 3 * d // 4
    assert mg_off % gw == 0 and gw % COL_CHUNK == 0
    row = lambda width: pl.BlockSpec((tm, width), lambda i: (i, 0))
    gate = lambda q: pl.BlockSpec((tm, gw), lambda i: (i, mg_off // gw + q))
    return pl.pallas_call(
        _merge_proj_kernel,
        grid=(t // tm,),
        in_specs=[pl.BlockSpec((ya.shape[0], tm, LANES), lambda i: (0, i, 0)),
                  pl.BlockSpec((tm, fw), lambda i: (i, fz_off // fw)), row(ub.shape[1]), row(uc.shape[1]),
                  gate(0), gate(1), gate(2), gate(3)] + [_layer_spec(a, l) for a in (wf, wp, wc)],
        out_specs=row(d),
        out_shape=jax.ShapeDtypeStruct((t, d), BF16),
        compiler_params=_cparams("parallel"),
        name="merge_proj",
    )(ya, h, ub, uc, h, h, h, h, wf, wp, wc)


def _out_ple_kernel(m_ref, x_ref, p_ref, wo_ref, bo_ref, g_ref, b_ref, wpl_ref, wg_ref, bg_ref, of_ref, ob_ref,
                    z_scr, xb_scr, *, alpha):
    d = x_ref.shape[1]
    nch = d // COL_CHUNK
    for c in range(nch):
        sl = slice(c * COL_CHUNK, (c + 1) * COL_CHUNK)
        out = jnp.dot(m_ref[...], wo_ref[:, sl], preferred_element_type=F32) + bo_ref[:, sl]
        z_scr[:, sl] = alpha * x_ref[:, sl] + out
    xn = _layer_norm(z_scr[...], g_ref[...], b_ref[...])
    z_scr[...] = xn
    xb_scr[...] = xn.astype(BF16)
    pb = p_ref[...].astype(BF16)
    for c in range(nch):
        sl = slice(c * COL_CHUNK, (c + 1) * COL_CHUNK)
        gate = _sigmoid(jnp.dot(xb_scr[...], wg_ref[:, sl], preferred_element_type=F32) + bg_ref[:, sl])
        ple = jnp.dot(pb, wpl_ref[:, sl], preferred_element_type=F32)
        y = z_scr[:, sl] + gate * ple
        of_ref[:, sl] = y
        ob_ref[:, sl] = y.astype(BF16)


def _out_ple(m, x, p, wo, bo, g, b, wpl, wg, bg, l, alpha, tm):
    t, d = x.shape
    row = lambda width: pl.BlockSpec((tm, width), lambda i: (i, 0))
    return pl.pallas_call(
        functools.partial(_out_ple_kernel, alpha=alpha),
        grid=(t // tm,),
        in_specs=[row(d), row(d), pl.BlockSpec((None, tm, p.shape[2]), lambda i: (l, i, 0))]
        + [_layer_spec(a, l) for a in (wo, bo, g, b, wpl, wg, bg)],
        out_specs=[row(d), row(d)],
        out_shape=[jax.ShapeDtypeStruct((t, d), F32), jax.ShapeDtypeStruct((t, d), BF16)],
        scratch_shapes=[pltpu.VMEM((tm, d), F32), pltpu.VMEM((tm, d), BF16)],
        compiler_params=_cparams("parallel"),
        name="out_ple",
    )(m, x, p, wo, bo, g, b, wpl, wg, bg)


def _angles(rows, cols, period):
    prod = (rows[:, None] * cols[None, :]) % period
    return prod.astype(F32) * (2.0 * math.pi / period)


def _dft_tables(s, n1, n2, gd):
    i1 = jnp.arange(n1, dtype=jnp.int32)
    a1 = _angles(i1, i1, n1)
    t1 = jnp.concatenate([jnp.cos(a1), -jnp.sin(a1)], axis=0) / math.sqrt(n1)
    t1 = jnp.kron(t1, jnp.eye(SUBLANES, dtype=F32)).astype(BF16)
    ic = jnp.arange(gd, dtype=jnp.int32)
    ac = _angles(ic, ic, gd)
    cc, sc = jnp.cos(ac) / math.sqrt(gd), jnp.sin(ac) / math.sqrt(gd)
    w2 = jnp.concatenate([jnp.concatenate([cc, -sc], axis=1), jnp.concatenate([sc, cc], axis=1)], axis=0)
    k = jnp.arange(s, dtype=jnp.int32)
    am = _angles(k, jnp.arange(n2, dtype=jnp.int32), s)
    m = jnp.concatenate([jnp.cos(am), jnp.sin(am)], axis=1) / math.sqrt(n2)
    mt = m.reshape(n2, n1, 2 * n2).transpose(1, 0, 2).astype(BF16)
    return t1, w2.astype(BF16), mt


def _pick(n, pref):
    t = min(n, pref)
    while n % t:
        t //= 2
    return t


def _trunk(x, p, prm, depth, d):
    bsz, s, _ = x.shape
    t = bsz * s
    fw, pw, cw = d // 4, d // 2, d // 4
    gd = fw // FNET_GROUPS
    assert gd == LANES and s % (2 * HALO) == 0
    segs = (("fv", fw, "none"), ("fz", fw, "silu"), ("pv", pw, "none"), ("pz", pw, "silu"),
            ("ca", cw, "none"), ("gg", cw, "sigmoid"), ("cz", cw, "silu"), ("mg", 3 * d, "none"))
    offs, piece_kinds, off = {}, [], 0
    for name, width, kind in segs:
        assert width % IN_COL_CHUNK == 0
        offs[name] = off
        off += width
        piece_kinds += [kind] * (width // IN_COL_CHUNK)
    tn = 3 * IN_COL_CHUNK
    assert off % tn == 0 and fw <= tn
    per_step = tn // IN_COL_CHUNK
    steps = [tuple(piece_kinds[i:i + per_step]) for i in range(0, len(piece_kinds), per_step)]
    groups = [[steps[0], 0, 1]]
    for j in range(1, len(steps)):
        if j > 1 and steps[j] == groups[-1][0]:
            groups[-1][2] = j + 1
        else:
            groups.append([steps[j], j, j + 1])
    step_groups = tuple(tuple(g) for g in groups)
    n_slabs = FNET_GROUPS

    n2 = 1 << ((s.bit_length() - 1 + 1) // 2)
    n1 = s // n2
    assert n1 * n2 == s and n1 % SUBLANES == 0 and n2 % SUBLANES == 0
    t1, w2, mt = _dft_tables(s, n1, n2, gd)
    alpha = (2.0 * depth) ** 0.25
    p = p.reshape(depth, t, -1)

    xf, xb = _emb_ln(x.reshape(t, d), prm["emb_ln_g"], prm["emb_ln_b"], _pick(t, 512))
    for l in range(depth):
        h, hf = _in_proj(xb, prm["w_in"], prm["b_in"], l, _pick(t, 2048), tn, n_slabs, step_groups)
        hh = _fnet_a(hf.reshape(n_slabs, bsz * n1, n2, LANES), t1, w2, bsz, n1, n2)
        ya = _fnet_b(hh, mt, bsz, n1, n2).reshape(FNET_GROUPS, t, LANES)
        ub, uc = _local_mix(h, prm["w_pool_group"], prm["b_pool_group"], prm["pool_scale"], prm["w_dw"],
                            prm["b_dw"], prm["conv_ln_g"], prm["conv_ln_b"], l, s, offs, _pick(s, 1024))
        m = _merge_proj(ya, ub, uc, h, prm["w_fnet_proj"], prm["w_pool_proj"], prm["w_conv_proj"], l,
                        offs["fz"], offs["mg"], _pick(t, 1024))
        xf, xb = _out_ple(m, xf, p, prm["w_out"], prm["b_out"], prm["ln_g"], prm["ln_b"], prm["w_ple"],
                          prm["w_ple_gate"], prm["b_ple_gate"], l, alpha, _pick(t, 512))
    return xf.reshape(bsz, s, d)


def kernel(x_prompt, x_sample, p_prompt, p_sample, emb_ln_g, emb_ln_b, w_in, b_in, w_fnet_proj, w_pool_group,
           b_pool_group, pool_scale, w_pool_proj, w_dw, b_dw, conv_ln_g, conv_ln_b, w_conv_proj, w_out, b_out,
           ln_g, ln_b, w_ple, w_ple_gate, b_ple_gate):
    depth, d, _ = w_in.shape
    pw = d // 2
    row = lambda a: a[:, None, :]
    prm = dict(
        emb_ln_g=emb_ln_g[None, :], emb_ln_b=emb_ln_b[None, :], w_in=w_in.astype(BF16), b_in=row(b_in),
        w_fnet_proj=w_fnet_proj.astype(BF16), w_pool_group=w_pool_group.astype(BF16),
        b_pool_group=b_pool_group.reshape(depth, 1, pw), pool_scale=row(pool_scale),
        w_pool_proj=w_pool_proj.astype(BF16), w_dw=w_dw, b_dw=row(b_dw), conv_ln_g=row(conv_ln_g),
        conv_ln_b=row(conv_ln_b), w_conv_proj=w_conv_proj.astype(BF16), w_out=w_out.astype(BF16),
        b_out=row(b_out), ln_g=row(ln_g), ln_b=row(ln_b), w_ple=w_ple.astype(BF16),
        w_ple_gate=w_ple_gate.astype(BF16), b_ple_gate=row(b_ple_gate))
    y_prompt = _trunk(x_prompt, p_prompt, prm, depth, d)
    y_sample = _trunk(x_sample, p_sample, prm, depth, d)
    return (y_prompt, y_sample)
```

```python
import functools
import math

import jax
import jax.numpy as jnp
from jax import lax
from jax.experimental import pallas as pl
from jax.experimental.pallas import tpu as pltpu

F32 = jnp.float32
BF16 = jnp.bfloat16

LN_EPS = 1e-5
FNET_GROUPS = 4
POOL_WINDOWS = (2, 4, 8, 16)
CONV_KERNEL = 31
HALO = 16
LANES = 128
SUBLANES = 8
FNET_ROWS = 16
COL_CHUNK = 512
IN_ROW_CHUNK = 512
IN_COL_CHUNK = 512

VMEM_LIMIT_BYTES = 56 * 1024 * 1024


def _cparams(*sem):
    return pltpu.CompilerParams(dimension_semantics=sem, vmem_limit_bytes=VMEM_LIMIT_BYTES)


def _const_spec(shape):
    nd = len(shape)
    return pl.BlockSpec(shape, lambda *_: (0,) * nd, pipeline_mode=pl.Buffered(1))


def _layer_spec(arr, l):
    tail = tuple(arr.shape[1:])
    return pl.BlockSpec((None,) + tail, lambda *_: (l,) + (0,) * len(tail), pipeline_mode=pl.Buffered(1))


def _sigmoid(x):
    return 0.5 * jnp.tanh(0.5 * x) + 0.5


def _layer_norm(x, g, b):
    mu = jnp.mean(x, axis=-1, keepdims=True)
    xc = x - mu
    var = jnp.mean(xc * xc, axis=-1, keepdims=True)
    return xc * lax.rsqrt(var + LN_EPS) * g + b


def _emb_ln_kernel(x_ref, g_ref, b_ref, of_ref, ob_ref):
    y = _layer_norm(x_ref[...], g_ref[...], b_ref[...])
    of_ref[...] = y
    ob_ref[...] = y.astype(BF16)


def _emb_ln(x, g, b, tm):
    t, d = x.shape
    return pl.pallas_call(
        _emb_ln_kernel,
        grid=(t // tm,),
        in_specs=[pl.BlockSpec((tm, d), lambda i: (i, 0)), _const_spec((1, d)), _const_spec((1, d))],
        out_specs=[pl.BlockSpec((tm, d), lambda i: (i, 0)), pl.BlockSpec((tm, d), lambda i: (i, 0))],
        out_shape=[jax.ShapeDtypeStruct((t, d), F32), jax.ShapeDtypeStruct((t, d), BF16)],
        compiler_params=_cparams("parallel"),
        name="emb_ln",
    )(x, g, b)


def _in_proj_kernel(x_ref, w_ref, b_ref, o_ref, hf_ref, *, step_groups):
    tm = o_ref.shape[0]
    n_slabs = hf_ref.shape[0]
    slabs_per_piece = IN_COL_CHUNK // LANES

    def body(kinds, keep_f32):
        for c, kind in enumerate(kinds):
            sl = slice(c * IN_COL_CHUNK, (c + 1) * IN_COL_CHUNK)
            for m in range(tm // IN_ROW_CHUNK):
                rows = slice(m * IN_ROW_CHUNK, (m + 1) * IN_ROW_CHUNK)
                acc = jnp.dot(x_ref[rows, :], w_ref[:, sl], preferred_element_type=F32) + b_ref[:, sl]
                if kind == "silu":
                    acc = acc * _sigmoid(acc)
                elif kind == "sigmoid":
                    acc = _sigmoid(acc)
                o_ref[rows, sl] = acc.astype(BF16)
                if keep_f32:
                    for s in range(slabs_per_piece):
                        if c * slabs_per_piece + s < n_slabs:
                            hf_ref[c * slabs_per_piece + s, rows, :] = acc[:, s * LANES:(s + 1) * LANES]

    j = pl.program_id(1)
    for kinds, lo, hi in step_groups:
        pl.when(jnp.logical_and(j >= lo, j < hi))(functools.partial(body, kinds, lo == 0))


def _in_proj(xb, w, b, l, tm, tn, n_slabs, step_groups):
    t, d = xb.shape
    n = w.shape[2]
    return pl.pallas_call(
        functools.partial(_in_proj_kernel, step_groups=step_groups),
        grid=(t // tm, n // tn),
        in_specs=[pl.BlockSpec((tm, d), lambda i, j: (i, 0)), pl.BlockSpec((None, d, tn), lambda i, j: (l, 0, j)),
                  pl.BlockSpec((None, 1, tn), lambda i, j: (l, 0, j))],
        out_specs=[pl.BlockSpec((tm, tn), lambda i, j: (i, j)),
                   pl.BlockSpec((n_slabs, tm, LANES), lambda i, j: (0, i, 0))],
        out_shape=[jax.ShapeDtypeStruct((t, n), BF16), jax.ShapeDtypeStruct((n_slabs, t, LANES), F32)],
        compiler_params=_cparams("parallel", "arbitrary"),
        name="in_proj",
    )(xb, w, b)


def _fnet_a_kernel(x_ref, t1_ref, w2_ref, o_ref):
    ng, n1, nr = x_ref.shape[0], x_ref.shape[1], x_ref.shape[2]
    rows = n1 * nr
    x = jnp.concatenate([x_ref[g].reshape(rows, LANES) for g in range(ng)], axis=1).astype(BF16)
    gm = jnp.dot(t1_ref[...], x, preferred_element_type=F32)
    for g in range(ng):
        ln = slice(g * LANES, (g + 1) * LANES)
        lhs = jnp.concatenate([gm[:rows, ln], gm[rows:, ln]], axis=1).astype(BF16)
        hm = jnp.dot(lhs, w2_ref[...], preferred_element_type=F32).astype(BF16)
        o_ref[g, 0] = hm[:, :LANES].reshape(n1, nr, LANES)
        o_ref[g, 1] = hm[:, LANES:].reshape(n1, nr, LANES)


def _fnet_a(hf4, t1, w2, bsz, n1, n2):
    ng = FNET_GROUPS
    return pl.pallas_call(
        _fnet_a_kernel,
        grid=(bsz, n2 // FNET_ROWS),
        in_specs=[pl.BlockSpec((ng, n1, FNET_ROWS, LANES), lambda b, j: (0, b, j, 0)),
                  _const_spec(t1.shape), _const_spec(w2.shape)],
        out_specs=pl.BlockSpec((ng, 2, n1, FNET_ROWS, LANES), lambda b, j: (0, 0, b, j, 0)),
        out_shape=jax.ShapeDtypeStruct((ng, 2, bsz * n1, n2, LANES), BF16),
        compiler_params=_cparams("parallel", "parallel"),
        name="fnet_a",
    )(hf4, t1, w2)


def _fnet_b_kernel(h_ref, m_ref, o_ref):
    ng = h_ref.shape[0]
    for kk in range(SUBLANES):
        hr = jnp.concatenate([h_ref[g, 0, kk] for g in range(ng)], axis=1)
        hi = jnp.concatenate([h_ref[g, 1, kk] for g in range(ng)], axis=1)
        rhs = jnp.concatenate([hr, hi], axis=0)
        y = jnp.dot(m_ref[kk], rhs, preferred_element_type=F32)
        for g in range(ng):
            o_ref[g, :, kk, :] = y[:, g * LANES:(g + 1) * LANES]


def _fnet_b(hh, mt, bsz, n1, n2):
    ng = FNET_GROUPS
    nk = n1 // SUBLANES
    return pl.pallas_call(
        _fnet_b_kernel,
        grid=(bsz, nk),
        in_specs=[pl.BlockSpec((ng, 2, SUBLANES, n2, LANES), lambda b, k: (0, 0, b * nk + k, 0, 0)),
                  pl.BlockSpec((SUBLANES, n2, 2 * n2), lambda b, k: (k, 0, 0))],
        out_specs=pl.BlockSpec((ng, n2, SUBLANES, LANES), lambda b, k: (0, b, k, 0)),
        out_shape=jax.ShapeDtypeStruct((ng, bsz * n2, n1, LANES), F32),
        compiler_params=_cparams("parallel", "parallel"),
        name="fnet_b",
    )(hh, mt)


def _local_mix_kernel(pv_m, pv_p, pv_n, ca_m, ca_p, ca_n, gg_m, gg_p, gg_n, pz_ref, cz_ref,
                      wg_ref, bg_ref, sc_ref, wdw_ref, bdw_ref, cg_ref, cb_ref,
                      ub_ref, uc_ref, pvs, vs, vsh, psum, *, ts, seq):
    i = pl.program_id(0)
    pos = (i * ts) % seq
    has_prev = pos > 0
    has_next = pos + ts < seq

    pvs[0:HALO, :] = jnp.where(has_prev, pv_p[...].astype(F32), 0.0)
    pvs[HALO:HALO + ts, :] = pv_m[...].astype(F32)
    pvs[HALO + ts:, :] = jnp.where(has_next, pv_n[...].astype(F32), 0.0)
    vs[0:HALO, :] = jnp.where(has_prev, ca_p[...].astype(F32) * gg_p[...].astype(F32), 0.0)
    vs[HALO:HALO + ts, :] = ca_m[...].astype(F32) * gg_m[...].astype(F32)
    vs[HALO + ts:, :] = jnp.where(has_next, ca_n[...].astype(F32) * gg_n[...].astype(F32), 0.0)

    pgd = pvs.shape[1] // len(POOL_WINDOWS)
    nrows = pvs.shape[0]
    psum[0:SUBLANES, :] = jnp.zeros((SUBLANES, pgd), F32)
    tpos = pos + lax.broadcasted_iota(jnp.int32, (ts, 1), 0)
    for gi, w in enumerate(POOL_WINDOWS):
        sl = slice(gi * pgd, (gi + 1) * pgd)
        lo = tpos - w // 2
        cnt = jnp.minimum(lo + w, seq) - jnp.maximum(lo, 0)
        inv_cnt = 1.0 / cnt.astype(F32)
        cur = pvs[SUBLANES:nrows, sl] + pvs[SUBLANES - 1:nrows - 1, sl]
        span = 2
        while span < w:
            psum[SUBLANES:nrows, :] = cur
            cur = cur + psum[SUBLANES - span:nrows - span, :]
            span *= 2
        psum[SUBLANES:nrows, :] = cur
        win = psum[HALO + w // 2 - 1:HALO + w // 2 - 1 + ts, :]
        pooled = win * inv_cnt - pvs[HALO:HALO + ts, sl]
        mixed = jnp.dot(pooled.astype(BF16), wg_ref[gi], preferred_element_type=F32) + bg_ref[:, sl]
        ub_ref[:, sl] = (mixed * sc_ref[:, sl] * pz_ref[:, sl].astype(F32)).astype(BF16)

    nsh = vsh.shape[1]
    for r in range(1, SUBLANES):
        vsh[r] = vs[r:r + nsh, :]
    half = CONV_KERNEL // 2
    acc = None
    for kk in range(CONV_KERNEL):
        q, r = divmod(HALO - half + kk, SUBLANES)
        rows = vs[q * SUBLANES:q * SUBLANES + ts, :] if r == 0 else vsh[r, q * SUBLANES:q * SUBLANES + ts, :]
        term = rows * wdw_ref[kk:kk + 1, :]
        acc = term if acc is None else acc + term
    acc = acc + bdw_ref[...]
    y = _layer_norm(acc, cg_ref[...], cb_ref[...])
    y = y * _sigmoid(y)
    uc_ref[...] = (y * cz_ref[...].astype(F32)).astype(BF16)


def _local_mix(h, wg, bg, sc, wdw, bdw, cg, cb, l, seq, offs, ts):
    t = h.shape[0]
    pw, cw = wg.shape[1] * wg.shape[2], wdw.shape[2]
    r = ts // HALO
    last = t // HALO - 1

    def main(width, off):
        return pl.BlockSpec((ts, width), lambda i: (i, off // width))

    def prev(width, off):
        return pl.BlockSpec((HALO, width), lambda i: (jnp.maximum(i * r - 1, 0), off // width))

    def nxt(width, off):
        return pl.BlockSpec((HALO, width), lambda i: (jnp.minimum((i + 1) * r, last), off // width))

    in_specs = []
    for width, off in ((pw, offs["pv"]), (cw, offs["ca"]), (cw, offs["gg"])):
        in_specs += [main(width, off), prev(width, off), nxt(width, off)]
    in_specs += [main(pw, offs["pz"]), main(cw, offs["cz"])]
    in_specs += [_layer_spec(a, l) for a in (wg, bg, sc, wdw, bdw, cg, cb)]
    sh_rows = ts + ((2 * HALO - 1) // SUBLANES) * SUBLANES
    return pl.pallas_call(
        functools.partial(_local_mix_kernel, ts=ts, seq=seq),
        grid=(t // ts,),
        in_specs=in_specs,
        out_specs=[pl.BlockSpec((ts, pw), lambda i: (i, 0)), pl.BlockSpec((ts, cw), lambda i: (i, 0))],
        out_shape=[jax.ShapeDtypeStruct((t, pw), BF16), jax.ShapeDtypeStruct((t, cw), BF16)],
        scratch_shapes=[pltpu.VMEM((ts + 2 * HALO, pw), F32), pltpu.VMEM((ts + 2 * HALO, cw), F32),
                        pltpu.VMEM((SUBLANES, sh_rows, cw), F32),
                        pltpu.VMEM((ts + 2 * HALO, pw // len(POOL_WINDOWS)), F32)],
        compiler_params=_cparams("parallel"),
        name="local_mix",
    )(*([h] * 11), wg, bg, sc, wdw, bdw, cg, cb)


def _merge_proj_kernel(ya_ref, fz_ref, ub_ref, uc_ref, g0_ref, g1_ref, g2_ref, g3_ref, wf_ref, wp_ref, wc_ref, m_ref):
    d = m_ref.shape[1]
    gate_refs = (g0_ref, g1_ref, g2_ref, g3_ref)
    per_ref = g0_ref.shape[1] // COL_CHUNK
    nch = d // COL_CHUNK

    def gate(q):
        return _sigmoid(
            gate_refs[q // per_ref][:, (q % per_ref) * COL_CHUNK:(q % per_ref + 1) * COL_CHUNK].astype(F32))

    ya = jnp.concatenate([ya_ref[g] for g in range(ya_ref.shape[0])], axis=1)
    ua = (ya * fz_ref[...].astype(F32)).astype(BF16)
    for c in range(nch):
        sl = slice(c * COL_CHUNK, (c + 1) * COL_CHUNK)
        m = gate(c) * jnp.dot(ua, wf_ref[:, sl], preferred_element_type=F32)
        m = m + gate(nch + c) * jnp.dot(ub_ref[...], wp_ref[:, sl], preferred_element_type=F32)
        m = m + gate(2 * nch + c) * jnp.dot(uc_ref[...], wc_ref[:, sl], preferred_element_type=F32)
        m_ref[:, sl] = m.astype(BF16)


def _merge_proj(ya, ub, uc, h, wf, wp, wc, l, fz_off, mg_off, tm):
    t = h.shape[0]
    d = wf.shape[2]
    fw = ya.shape[0] * LANES
    gw = 3 * d // 4
    assert mg_off % gw == 0 and gw % COL_CHUNK == 0
    row = lambda width: pl.BlockSpec((tm, width), lambda i: (i, 0))
    gate = lambda q: pl.BlockSpec((tm, gw), lambda i: (i, mg_off // gw + q))
    return pl.pallas_call(
        _merge_proj_kernel,
        grid=(t // tm,),
        in_specs=[pl.BlockSpec((ya.shape[0], tm, LANES), lambda i: (0, i, 0)),
                  pl.BlockSpec((tm, fw), lambda i: (i, fz_off // fw)), row(ub.shape[1]), row(uc.shape[1]),
                  gate(0), gate(1), gate(2), gate(3)] + [_layer_spec(a, l) for a in (wf, wp, wc)],
        out_specs=row(d),
        out_shape=jax.ShapeDtypeStruct((t, d), BF16),
        compiler_params=_cparams("parallel"),
        name="merge_proj",
    )(ya, h, ub, uc, h, h, h, h, wf, wp, wc)


def _out_ple_kernel(m_ref, x_ref, p_ref, wo_ref, bo_ref, g_ref, b_ref, wpl_ref, wg_ref, bg_ref, of_ref, ob_ref,
                    z_scr, xb_scr, *, alpha):
    d = x_ref.shape[1]
    nch = d // COL_CHUNK
    for c in range(nch):
        sl = slice(c * COL_CHUNK, (c + 1) * COL_CHUNK)
        out = jnp.dot(m_ref[...], wo_ref[:, sl], preferred_element_type=F32) + bo_ref[:, sl]
        z_scr[:, sl] = alpha * x_ref[:, sl] + out
    xn = _layer_norm(z_scr[...], g_ref[...], b_ref[...])
    z_scr[...] = xn
    xb_scr[...] = xn.astype(BF16)
    pb = p_ref[...].astype(BF16)
    for c in range(nch):
        sl = slice(c * COL_CHUNK, (c + 1) * COL_CHUNK)
        gate = _sigmoid(jnp.dot(xb_scr[...], wg_ref[:, sl], preferred_element_type=F32) + bg_ref[:, sl])
        ple = jnp.dot(pb, wpl_ref[:, sl], preferred_element_type=F32)
        y = z_scr[:, sl] + gate * ple
        of_ref[:, sl] = y
        ob_ref[:, sl] = y.astype(BF16)


def _out_ple(m, x, p, wo, bo, g, b, wpl, wg, bg, l, alpha, tm):
    t, d = x.shape
    row = lambda width: pl.BlockSpec((tm, width), lambda i: (i, 0))
    return pl.pallas_call(
        functools.partial(_out_ple_kernel, alpha=alpha),
        grid=(t // tm,),
        in_specs=[row(d), row(d), pl.BlockSpec((None, tm, p.shape[2]), lambda i: (l, i, 0))]
        + [_layer_spec(a, l) for a in (wo, bo, g, b, wpl, wg, bg)],
        out_specs=[row(d), row(d)],
        out_shape=[jax.ShapeDtypeStruct((t, d), F32), jax.ShapeDtypeStruct((t, d), BF16)],
        scratch_shapes=[pltpu.VMEM((tm, d), F32), pltpu.VMEM((tm, d), BF16)],
        compiler_params=_cparams("parallel"),
        name="out_ple",
    )(m, x, p, wo, bo, g, b, wpl, wg, bg)


def _angles(rows, cols, period):
    prod = (rows[:, None] * cols[None, :]) % period
    return prod.astype(F32) * (2.0 * math.pi / period)


def _dft_tables(s, n1, n2, gd):
    i1 = jnp.arange(n1, dtype=jnp.int32)
    a1 = _angles(i1, i1, n1)
    t1 = jnp.concatenate([jnp.cos(a1), -jnp.sin(a1)], axis=0) / math.sqrt(n1)
    t1 = jnp.kron(t1, jnp.eye(FNET_ROWS, dtype=F32)).astype(BF16)
    ic = jnp.arange(gd, dtype=jnp.int32)
    ac = _angles(ic, ic, gd)
    cc, sc = jnp.cos(ac) / math.sqrt(gd), jnp.sin(ac) / math.sqrt(gd)
    w2 = jnp.concatenate([jnp.concatenate([cc, -sc], axis=1), jnp.concatenate([sc, cc], axis=1)], axis=0)
    k = jnp.arange(s, dtype=jnp.int32)
    am = _angles(k, jnp.arange(n2, dtype=jnp.int32), s)
    m = jnp.concatenate([jnp.cos(am), jnp.sin(am)], axis=1) / math.sqrt(n2)
    mt = m.reshape(n2, n1, 2 * n2).transpose(1, 0, 2).astype(BF16)
    return t1, w2.astype(BF16), mt


def _pick(n, pref):
    t = min(n, pref)
    while n % t:
        t //= 2
    return t


def _trunk(x, p, prm, depth, d):
    bsz, s, _ = x.shape
    t = bsz * s
    fw, pw, cw = d // 4, d // 2, d // 4
    gd = fw // FNET_GROUPS
    assert gd == LANES and s % (2 * HALO) == 0
    segs = (("fv", fw, "none"), ("fz", fw, "silu"), ("pv", pw, "none"), ("pz", pw, "silu"),
            ("ca", cw, "none"), ("gg", cw, "sigmoid"), ("cz", cw, "silu"), ("mg", 3 * d, "none"))
    offs, piece_kinds, off = {}, [], 0
    for name, width, kind in segs:
        assert width % IN_COL_CHUNK == 0
        offs[name] = off
        off += width
        piece_kinds += [kind] * (width // IN_COL_CHUNK)
    tn = 3 * IN_COL_CHUNK
    assert off % tn == 0 and fw <= tn
    per_step = tn // IN_COL_CHUNK
    steps = [tuple(piece_kinds[i:i + per_step]) for i in range(0, len(piece_kinds), per_step)]
    groups = [[steps[0], 0, 1]]
    for j in range(1, len(steps)):
        if j > 1 and steps[j] == groups[-1][0]:
            groups[-1][2] = j + 1
        else:
            groups.append([steps[j], j, j + 1])
    step_groups = tuple(tuple(g) for g in groups)
    n_slabs = FNET_GROUPS

    n2 = 1 << ((s.bit_length() - 1 + 1) // 2)
    n1 = s // n2
    assert n1 * n2 == s and n1 % SUBLANES == 0 and n2 % FNET_ROWS == 0
    t1, w2, mt = _dft_tables(s, n1, n2, gd)
    alpha = (2.0 * depth) ** 0.25
    p = p.reshape(depth, t, -1)

    xf, xb = _emb_ln(x.reshape(t, d), prm["emb_ln_g"], prm["emb_ln_b"], _pick(t, 512))
    for l in range(depth):
        h, hf = _in_proj(xb, prm["w_in"], prm["b_in"], l, _pick(t, 2048), tn, n_slabs, step_groups)
        hh = _fnet_a(hf.reshape(n_slabs, bsz * n1, n2, LANES), t1, w2, bsz, n1, n2)
        ya = _fnet_b(hh, mt, bsz, n1, n2).reshape(FNET_GROUPS, t, LANES)
        ub, uc = _local_mix(h, prm["w_pool_group"], prm["b_pool_group"], prm["pool_scale"], prm["w_dw"],
                            prm["b_dw"], prm["conv_ln_g"], prm["conv_ln_b"], l, s, offs, _pick(s, 1024))
        m = _merge_proj(ya, ub, uc, h, prm["w_fnet_proj"], prm["w_pool_proj"], prm["w_conv_proj"], l,
                        offs["fz"], offs["mg"], _pick(t, 1024))
        xf, xb = _out_ple(m, xf, p, prm["w_out"], prm["b_out"], prm["ln_g"], prm["ln_b"], prm["w_ple"],
                          prm["w_ple_gate"], prm["b_ple_gate"], l, alpha, _pick(t, 512))
    return xf.reshape(bsz, s, d)


def kernel(x_prompt, x_sample, p_prompt, p_sample, emb_ln_g, emb_ln_b, w_in, b_in, w_fnet_proj, w_pool_group,
           b_pool_group, pool_scale, w_pool_proj, w_dw, b_dw, conv_ln_g, conv_ln_b, w_conv_proj, w_out, b_out,
           ln_g, ln_b, w_ple, w_ple_gate, b_ple_gate):
    depth, d, _ = w_in.shape
    pw = d // 2
    row = lambda a: a[:, None, :]
    prm = dict(
        emb_ln_g=emb_ln_g[None, :], emb_ln_b=emb_ln_b[None, :], w_in=w_in.astype(BF16), b_in=row(b_in),
        w_fnet_proj=w_fnet_proj.astype(BF16), w_pool_group=w_pool_group.astype(BF16),
        b_pool_group=b_pool_group.reshape(depth, 1, pw), pool_scale=row(pool_scale),
        w_pool_proj=w_pool_proj.astype(BF16), w_dw=w_dw, b_dw=row(b_dw), conv_ln_g=row(conv_ln_g),
        conv_ln_b=row(conv_ln_b), w_conv_proj=w_conv_proj.astype(BF16), w_out=w_out.astype(BF16),
        b_out=row(b_out), ln_g=row(ln_g), ln_b=row(ln_b), w_ple=w_ple.astype(BF16),
        w_ple_gate=w_ple_gate.astype(BF16), b_ple_gate=row(b_ple_gate))
    y_prompt = _trunk(x_prompt, p_prompt, prm, depth, d)
    y_sample = _trunk(x_sample, p_sample, prm, depth, d)
    return (y_prompt, y_sample)
```

```python
import functools
import math

import jax
import jax.numpy as jnp
from jax import lax
from jax.experimental import pallas as pl
from jax.experimental.pallas import tpu as pltpu

F32 = jnp.float32
BF16 = jnp.bfloat16

LN_EPS = 1e-5
FNET_GROUPS = 4
POOL_WINDOWS = (2, 4, 8, 16)
CONV_KERNEL = 31
HALO = 16
LANES = 128
SUBLANES = 8
FNET_ROWS = 16
COL_CHUNK = 512
IN_ROW_CHUNK = 512
IN_COL_CHUNK = 512

VMEM_LIMIT_BYTES = 56 * 1024 * 1024


def _cparams(*sem):
    return pltpu.CompilerParams(dimension_semantics=sem, vmem_limit_bytes=VMEM_LIMIT_BYTES)


def _const_spec(shape):
    nd = len(shape)
    return pl.BlockSpec(shape, lambda *_: (0,) * nd, pipeline_mode=pl.Buffered(1))


def _layer_spec(arr, l):
    tail = tuple(arr.shape[1:])
    return pl.BlockSpec((None,) + tail, lambda *_: (l,) + (0,) * len(tail), pipeline_mode=pl.Buffered(1))


def _sigmoid(x):
    return 0.5 * jnp.tanh(0.5 * x) + 0.5


def _layer_norm(x, g, b):
    mu = jnp.mean(x, axis=-1, keepdims=True)
    xc = x - mu
    var = jnp.mean(xc * xc, axis=-1, keepdims=True)
    return xc * lax.rsqrt(var + LN_EPS) * g + b


def _emb_ln_kernel(x_ref, g_ref, b_ref, of_ref, ob_ref):
    y = _layer_norm(x_ref[...], g_ref[...], b_ref[...])
    of_ref[...] = y
    ob_ref[...] = y.astype(BF16)


def _emb_ln(x, g, b, tm):
    t, d = x.shape
    return pl.pallas_call(
        _emb_ln_kernel,
        grid=(t // tm,),
        in_specs=[pl.BlockSpec((tm, d), lambda i: (i, 0)), _const_spec((1, d)), _const_spec((1, d))],
        out_specs=[pl.BlockSpec((tm, d), lambda i: (i, 0)), pl.BlockSpec((tm, d), lambda i: (i, 0))],
        out_shape=[jax.ShapeDtypeStruct((t, d), F32), jax.ShapeDtypeStruct((t, d), BF16)],
        compiler_params=_cparams("parallel"),
        name="emb_ln",
    )(x, g, b)


def _in_proj_kernel(x_ref, w_ref, b_ref, o_ref, hf_ref, *, step_groups):
    tm = o_ref.shape[0]
    n_slabs = hf_ref.shape[0]
    slabs_per_piece = IN_COL_CHUNK // LANES

    def body(kinds, keep_f32):
        for c, kind in enumerate(kinds):
            sl = slice(c * IN_COL_CHUNK, (c + 1) * IN_COL_CHUNK)
            for m in range(tm // IN_ROW_CHUNK):
                rows = slice(m * IN_ROW_CHUNK, (m + 1) * IN_ROW_CHUNK)
                acc = jnp.dot(x_ref[rows, :], w_ref[:, sl], preferred_element_type=F32) + b_ref[:, sl]
                if kind == "silu":
                    acc = acc * _sigmoid(acc)
                elif kind == "sigmoid":
                    acc = _sigmoid(acc)
                o_ref[rows, sl] = acc.astype(BF16)
                if keep_f32:
                    for s in range(slabs_per_piece):
                        if c * slabs_per_piece + s < n_slabs:
                            hf_ref[c * slabs_per_piece + s, rows, :] = acc[:, s * LANES:(s + 1) * LANES]

    j = pl.program_id(1)
    for kinds, lo, hi in step_groups:
        pl.when(jnp.logical_and(j >= lo, j < hi))(functools.partial(body, kinds, lo == 0))


def _in_proj(xb, w, b, l, tm, tn, n_slabs, step_groups):
    t, d = xb.shape
    n = w.shape[2]
    return pl.pallas_call(
        functools.partial(_in_proj_kernel, step_groups=step_groups),
        grid=(t // tm, n // tn),
        in_specs=[pl.BlockSpec((tm, d), lambda i, j: (i, 0)), pl.BlockSpec((None, d, tn), lambda i, j: (l, 0, j)),
                  pl.BlockSpec((None, 1, tn), lambda i, j: (l, 0, j))],
        out_specs=[pl.BlockSpec((tm, tn), lambda i, j: (i, j)),
                   pl.BlockSpec((n_slabs, tm, LANES), lambda i, j: (0, i, 0))],
        out_shape=[jax.ShapeDtypeStruct((t, n), BF16), jax.ShapeDtypeStruct((n_slabs, t, LANES), F32)],
        compiler_params=_cparams("parallel", "arbitrary"),
        name="in_proj",
    )(xb, w, b)


def _fnet_a_kernel(x_ref, t1_ref, w2_ref, o_ref):
    ng, n1, nr = x_ref.shape[0], x_ref.shape[1], x_ref.shape[2]
    rows = n1 * nr
    x = jnp.concatenate([x_ref[g].reshape(rows, LANES) for g in range(ng)], axis=1).astype(BF16)
    gm = jnp.dot(t1_ref[...], x, preferred_element_type=F32)
    for g in range(ng):
        ln = slice(g * LANES, (g + 1) * LANES)
        lhs = jnp.concatenate([gm[:rows, ln], gm[rows:, ln]], axis=1).astype(BF16)
        hm = jnp.dot(lhs, w2_ref[...], preferred_element_type=F32).astype(BF16)
        o_ref[g, 0] = hm[:, :LANES].reshape(n1, nr, LANES)
        o_ref[g, 1] = hm[:, LANES:].reshape(n1, nr, LANES)


def _fnet_a(hf4, t1, w2, bsz, n1, n2):
    ng = FNET_GROUPS
    return pl.pallas_call(
        _fnet_a_kernel,
        grid=(bsz, n2 // FNET_ROWS),
        in_specs=[pl.BlockSpec((ng, n1, FNET_ROWS, LANES), lambda b, j: (0, b, j, 0)),
                  _const_spec(t1.shape), _const_spec(w2.shape)],
        out_specs=pl.BlockSpec((ng, 2, n1, FNET_ROWS, LANES), lambda b, j: (0, 0, b, j, 0)),
        out_shape=jax.ShapeDtypeStruct((ng, 2, bsz * n1, n2, LANES), BF16),
        compiler_params=_cparams("parallel", "parallel"),
        name="fnet_a",
    )(hf4, t1, w2)


def _fnet_b_kernel(h_ref, m_ref, o_ref):
    ng = h_ref.shape[0]
    for kk in range(SUBLANES):
        hr = jnp.concatenate([h_ref[g, 0, kk] for g in range(ng)], axis=1)
        hi = jnp.concatenate([h_ref[g, 1, kk] for g in range(ng)], axis=1)
        rhs = jnp.concatenate([hr, hi], axis=0)
        y = jnp.dot(m_ref[kk], rhs, preferred_element_type=F32)
        for g in range(ng):
            o_ref[g, :, kk, :] = y[:, g * LANES:(g + 1) * LANES]


def _fnet_b(hh, mt, bsz, n1, n2):
    ng = FNET_GROUPS
    nk = n1 // SUBLANES
    return pl.pallas_call(
        _fnet_b_kernel,
        grid=(bsz, nk),
        in_specs=[pl.BlockSpec((ng, 2, SUBLANES, n2, LANES), lambda b, k: (0, 0, b * nk + k, 0, 0)),
                  pl.BlockSpec((SUBLANES, n2, 2 * n2), lambda b, k: (k, 0, 0))],
        out_specs=pl.BlockSpec((ng, n2, SUBLANES, LANES), lambda b, k: (0, b, k, 0)),
        out_shape=jax.ShapeDtypeStruct((ng, bsz * n2, n1, LANES), F32),
        compiler_params=_cparams("parallel", "parallel"),
        name="fnet_b",
    )(hh, mt)


def _local_mix_kernel(pv_m, pv_p, pv_n, ca_m, ca_p, ca_n, gg_m, gg_p, gg_n, pz_ref, cz_ref,
                      wg_ref, bg_ref, sc_ref, wdw_ref, bdw_ref, cg_ref, cb_ref,
                      ub_ref, uc_ref, pvs, vs, vsh, psum, *, ts, seq):
    i = pl.program_id(0)
    pos = (i * ts) % seq
    has_prev = pos > 0
    has_next = pos + ts < seq

    pvs[0:HALO, :] = jnp.where(has_prev, pv_p[...].astype(F32), 0.0)
    pvs[HALO:HALO + ts, :] = pv_m[...].astype(F32)
    pvs[HALO + ts:, :] = jnp.where(has_next, pv_n[...].astype(F32), 0.0)
    vs[0:HALO, :] = jnp.where(has_prev, ca_p[...].astype(F32) * gg_p[...].astype(F32), 0.0)
    vs[HALO:HALO + ts, :] = ca_m[...].astype(F32) * gg_m[...].astype(F32)
    vs[HALO + ts:, :] = jnp.where(has_next, ca_n[...].astype(F32) * gg_n[...].astype(F32), 0.0)

    pgd = pvs.shape[1] // len(POOL_WINDOWS)
    nrows = pvs.shape[0]
    psum[0:SUBLANES, :] = jnp.zeros((SUBLANES, pgd), F32)
    tpos = pos + lax.broadcasted_iota(jnp.int32, (ts, 1), 0)
    for gi, w in enumerate(POOL_WINDOWS):
        sl = slice(gi * pgd, (gi + 1) * pgd)
        lo = tpos - w // 2
        cnt = jnp.minimum(lo + w, seq) - jnp.maximum(lo, 0)
        inv_cnt = 1.0 / cnt.astype(F32)
        cur = pvs[SUBLANES:nrows, sl] + pvs[SUBLANES - 1:nrows - 1, sl]
        span = 2
        while span < w:
            psum[SUBLANES:nrows, :] = cur
            cur = cur + psum[SUBLANES - span:nrows - span, :]
            span *= 2
        psum[SUBLANES:nrows, :] = cur
        win = psum[HALO + w // 2 - 1:HALO + w // 2 - 1 + ts, :]
        pooled = win * inv_cnt - pvs[HALO:HALO + ts, sl]
        mixed = jnp.dot(pooled.astype(BF16), wg_ref[gi], preferred_element_type=F32) + bg_ref[:, sl]
        ub_ref[:, sl] = (mixed * sc_ref[:, sl] * pz_ref[:, sl].astype(F32)).astype(BF16)

    nsh = vsh.shape[1]
    for r in range(1, SUBLANES):
        vsh[r] = vs[r:r + nsh, :]
    half = CONV_KERNEL // 2
    acc = None
    for kk in range(CONV_KERNEL):
        q, r = divmod(HALO - half + kk, SUBLANES)
        rows = vs[q * SUBLANES:q * SUBLANES + ts, :] if r == 0 else vsh[r, q * SUBLANES:q * SUBLANES + ts, :]
        term = rows * wdw_ref[kk:kk + 1, :]
        acc = term if acc is None else acc + term
    acc = acc + bdw_ref[...]
    y = _layer_norm(acc, cg_ref[...], cb_ref[...])
    y = y * _sigmoid(y)
    uc_ref[...] = (y * cz_ref[...].astype(F32)).astype(BF16)


def _local_mix(h, wg, bg, sc, wdw, bdw, cg, cb, l, seq, offs, ts):
    t = h.shape[0]
    pw, cw = wg.shape[1] * wg.shape[2], wdw.shape[2]
    r = ts // HALO
    last = t // HALO - 1

    def main(width, off):
        return pl.BlockSpec((ts, width), lambda i: (i, off // width))

    def prev(width, off):
        return pl.BlockSpec((HALO, width), lambda i: (jnp.maximum(i * r - 1, 0), off // width))

    def nxt(width, off):
        return pl.BlockSpec((HALO, width), lambda i: (jnp.minimum((i + 1) * r, last), off // width))

    in_specs = []
    for width, off in ((pw, offs["pv"]), (cw, offs["ca"]), (cw, offs["gg"])):
        in_specs += [main(width, off), prev(width, off), nxt(width, off)]
    in_specs += [main(pw, offs["pz"]), main(cw, offs["cz"])]
    in_specs += [_layer_spec(a, l) for a in (wg, bg, sc, wdw, bdw, cg, cb)]
    sh_rows = ts + ((2 * HALO - 1) // SUBLANES) * SUBLANES
    return pl.pallas_call(
        functools.partial(_local_mix_kernel, ts=ts, seq=seq),
        grid=(t // ts,),
        in_specs=in_specs,
        out_specs=[pl.BlockSpec((ts, pw), lambda i: (i, 0)), pl.BlockSpec((ts, cw), lambda i: (i, 0))],
        out_shape=[jax.ShapeDtypeStruct((t, pw), BF16), jax.ShapeDtypeStruct((t, cw), BF16)],
        scratch_shapes=[pltpu.VMEM((ts + 2 * HALO, pw), F32), pltpu.VMEM((ts + 2 * HALO, cw), F32),
                        pltpu.VMEM((SUBLANES, sh_rows, cw), F32),
                        pltpu.VMEM((ts + 2 * HALO, pw // len(POOL_WINDOWS)), F32)],
        compiler_params=_cparams("parallel"),
        name="local_mix",
    )(*([h] * 11), wg, bg, sc, wdw, bdw, cg, cb)


def _merge_proj_kernel(ya_ref, fz_ref, ub_ref, uc_ref, g0_ref, g1_ref, g2_ref, g3_ref, wf_ref, wp_ref, wc_ref, m_ref):
    d = m_ref.shape[1]
    gate_refs = (g0_ref, g1_ref, g2_ref, g3_ref)
    per_ref = g0_ref.shape[1] // COL_CHUNK
    nch = d // COL_CHUNK

    def gate(q):
        return _sigmoid(
            gate_refs[q // per_ref][:, (q % per_ref) * COL_CHUNK:(q % per_ref + 1) * COL_CHUNK].astype(F32))

    ya = jnp.concatenate([ya_ref[g] for g in range(ya_ref.shape[0])], axis=1)
    ua = (ya * fz_ref[...].astype(F32)).astype(BF16)
    for c in range(nch):
        sl = slice(c * COL_CHUNK, (c + 1) * COL_CHUNK)
        m = gate(c) * jnp.dot(ua, wf_ref[:, sl], preferred_element_type=F32)
        m = m + gate(nch + c) * jnp.dot(ub_ref[...], wp_ref[:, sl], preferred_element_type=F32)
        m = m + gate(2 * nch + c) * jnp.dot(uc_ref[...], wc_ref[:, sl], preferred_element_type=F32)
        m_ref[:, sl] = m.astype(BF16)


def _merge_proj(ya, ub, uc, h, wf, wp, wc, l, fz_off, mg_off, tm):
    t = h.shape[0]
    d = wf.shape[2]
    fw = ya.shape[0] * LANES
    gw = 3 * d // 4
    assert mg_off % gw == 0 and gw % COL_CHUNK == 0
    row = lambda width: pl.BlockSpec((tm, width), lambda i: (i, 0))
    gate = lambda q: pl.BlockSpec((tm, gw), lambda i: (i, mg_off // gw + q))
    return pl.pallas_call(
        _merge_proj_kernel,
        grid=(t // tm,),
        in_specs=[pl.BlockSpec((ya.shape[0], tm, LANES), lambda i: (0, i, 0)),
                  pl.BlockSpec((tm, fw), lambda i: (i, fz_off // fw)), row(ub.shape[1]), row(uc.shape[1]),
                  gate(0), gate(1), gate(2), gate(3)] + [_layer_spec(a, l) for a in (wf, wp, wc)],
        out_specs=row(d),
        out_shape=jax.ShapeDtypeStruct((t, d), BF16),
        compiler_params=_cparams("parallel"),
        name="merge_proj",
    )(ya, h, ub, uc, h, h, h, h, wf, wp, wc)


def _out_ple_kernel(m_ref, x_ref, p_ref, wo_ref, bo_ref, g_ref, b_ref, wpl_ref, wg_ref, bg_ref, of_ref, ob_ref,
                    z_scr, xb_scr, *, alpha):
    d = x_ref.shape[1]
    nch = d // COL_CHUNK
    for c in range(nch):
        sl = slice(c * COL_CHUNK, (c + 1) * COL_CHUNK)
        out = jnp.dot(m_ref[...], wo_ref[:, sl], preferred_element_type=F32) + bo_ref[:, sl]
        z_scr[:, sl] = alpha * x_ref[:, sl] + out
    xn = _layer_norm(z_scr[...], g_ref[...], b_ref[...])
    z_scr[...] = xn
    xb_scr[...] = xn.astype(BF16)
    pb = p_ref[...].astype(BF16)
    for c in range(nch):
        sl = slice(c * COL_CHUNK, (c + 1) * COL_CHUNK)
        gate = _sigmoid(jnp.dot(xb_scr[...], wg_ref[:, sl], preferred_element_type=F32) + bg_ref[:, sl])
        ple = jnp.dot(pb, wpl_ref[:, sl], preferred_element_type=F32)
        y = z_scr[:, sl] + gate * ple
        of_ref[:, sl] = y
        ob_ref[:, sl] = y.astype(BF16)


def _out_ple(m, x, p, wo, bo, g, b, wpl, wg, bg, l, alpha, tm):
    t, d = x.shape
    row = lambda width: pl.BlockSpec((tm, width), lambda i: (i, 0))
    return pl.pallas_call(
        functools.partial(_out_ple_kernel, alpha=alpha),
        grid=(t // tm,),
        in_specs=[row(d), row(d), pl.BlockSpec((None, tm, p.shape[2]), lambda i: (l, i, 0))]
        + [_layer_spec(a, l) for a in (wo, bo, g, b, wpl, wg, bg)],
        out_specs=[row(d), row(d)],
        out_shape=[jax.ShapeDtypeStruct((t, d), F32), jax.ShapeDtypeStruct((t, d), BF16)],
        scratch_shapes=[pltpu.VMEM((tm, d), F32), pltpu.VMEM((tm, d), BF16)],
        compiler_params=_cparams("parallel"),
        name="out_ple",
    )(m, x, p, wo, bo, g, b, wpl, wg, bg)


def _angles(rows, cols, period):
    prod = (rows[:, None] * cols[None, :]) % period
    return prod.astype(F32) * (2.0 * math.pi / period)


def _dft_tables(s, n1, n2, gd):
    ir = jnp.arange(n1 * FNET_ROWS, dtype=jnp.int32)
    a1 = _angles(ir // FNET_ROWS, ir // FNET_ROWS, n1)
    same = (ir % FNET_ROWS)[:, None] == (ir % FNET_ROWS)[None, :]
    t1 = jnp.concatenate([jnp.where(same, jnp.cos(a1), 0.0), jnp.where(same, -jnp.sin(a1), 0.0)], axis=0)
    t1 = (t1 / math.sqrt(n1)).astype(BF16)
    ic = jnp.arange(gd, dtype=jnp.int32)
    ac = _angles(ic, ic, gd)
    cc, sc = jnp.cos(ac) / math.sqrt(gd), jnp.sin(ac) / math.sqrt(gd)
    w2 = jnp.concatenate([jnp.concatenate([cc, -sc], axis=1), jnp.concatenate([sc, cc], axis=1)], axis=0)
    k = (jnp.arange(n1, dtype=jnp.int32)[:, None] + n1 * jnp.arange(n2, dtype=jnp.int32)[None, :]).reshape(-1)
    am = _angles(k, jnp.arange(n2, dtype=jnp.int32), s).reshape(n1, n2, n2)
    mt = (jnp.concatenate([jnp.cos(am), jnp.sin(am)], axis=2) / math.sqrt(n2)).astype(BF16)
    return t1, w2.astype(BF16), mt


def _pick(n, pref):
    t = min(n, pref)
    while n % t:
        t //= 2
    return t


def _trunk(x, p, prm, depth, d):
    bsz, s, _ = x.shape
    t = bsz * s
    fw, pw, cw = d // 4, d // 2, d // 4
    gd = fw // FNET_GROUPS
    assert gd == LANES and s % (2 * HALO) == 0
    segs = (("fv", fw, "none"), ("fz", fw, "silu"), ("pv", pw, "none"), ("pz", pw, "silu"),
            ("ca", cw, "none"), ("gg", cw, "sigmoid"), ("cz", cw, "silu"), ("mg", 3 * d, "none"))
    offs, piece_kinds, off = {}, [], 0
    for name, width, kind in segs:
        assert width % IN_COL_CHUNK == 0
        offs[name] = off
        off += width
        piece_kinds += [kind] * (width // IN_COL_CHUNK)
    tn = 3 * IN_COL_CHUNK
    assert off % tn == 0 and fw <= tn
    per_step = tn // IN_COL_CHUNK
    steps = [tuple(piece_kinds[i:i + per_step]) for i in range(0, len(piece_kinds), per_step)]
    groups = [[steps[0], 0, 1]]
    for j in range(1, len(steps)):
        if j > 1 and steps[j] == groups[-1][0]:
            groups[-1][2] = j + 1
        else:
            groups.append([steps[j], j, j + 1])
    step_groups = tuple(tuple(g) for g in groups)
    n_slabs = FNET_GROUPS

    n2 = 1 << ((s.bit_length() - 1 + 1) // 2)
    n1 = s // n2
    assert n1 * n2 == s and n1 % SUBLANES == 0 and n2 % FNET_ROWS == 0
    t1, w2, mt = _dft_tables(s, n1, n2, gd)
    alpha = (2.0 * depth) ** 0.25
    p = p.reshape(depth, t, -1)

    xf, xb = _emb_ln(x.reshape(t, d), prm["emb_ln_g"], prm["emb_ln_b"], _pick(t, 512))
    for l in range(depth):
        h, hf = _in_proj(xb, prm["w_in"], prm["b_in"], l, _pick(t, 2048), tn, n_slabs, step_groups)
        hh = _fnet_a(hf.reshape(n_slabs, bsz * n1, n2, LANES), t1, w2, bsz, n1, n2)
        ya = _fnet_b(hh, mt, bsz, n1, n2).reshape(FNET_GROUPS, t, LANES)
        ub, uc = _local_mix(h, prm["w_pool_group"], prm["b_pool_group"], prm["pool_scale"], prm["w_dw"],
                            prm["b_dw"], prm["conv_ln_g"], prm["conv_ln_b"], l, s, offs, _pick(s, 1024))
        m = _merge_proj(ya, ub, uc, h, prm["w_fnet_proj"], prm["w_pool_proj"], prm["w_conv_proj"], l,
                        offs["fz"], offs["mg"], _pick(t, 1024))
        xf, xb = _out_ple(m, xf, p, prm["w_out"], prm["b_out"], prm["ln_g"], prm["ln_b"], prm["w_ple"],
                          prm["w_ple_gate"], prm["b_ple_gate"], l, alpha, _pick(t, 512))
    return xf.reshape(bsz, s, d)


def kernel(x_prompt, x_sample, p_prompt, p_sample, emb_ln_g, emb_ln_b, w_in, b_in, w_fnet_proj, w_pool_group,
           b_pool_group, pool_scale, w_pool_proj, w_dw, b_dw, conv_ln_g, conv_ln_b, w_conv_proj, w_out, b_out,
           ln_g, ln_b, w_ple, w_ple_gate, b_ple_gate):
    depth, d, _ = w_in.shape
    pw = d // 2
    row = lambda a: a[:, None, :]
    prm = dict(
        emb_ln_g=emb_ln_g[None, :], emb_ln_b=emb_ln_b[None, :], w_in=w_in.astype(BF16), b_in=row(b_in),
        w_fnet_proj=w_fnet_proj.astype(BF16), w_pool_group=w_pool_group.astype(BF16),
        b_pool_group=b_pool_group.reshape(depth, 1, pw), pool_scale=row(pool_scale),
        w_pool_proj=w_pool_proj.astype(BF16), w_dw=w_dw, b_dw=row(b_dw), conv_ln_g=row(conv_ln_g),
        conv_ln_b=row(conv_ln_b), w_conv_proj=w_conv_proj.astype(BF16), w_out=w_out.astype(BF16),
        b_out=row(b_out), ln_g=row(ln_g), ln_b=row(ln_b), w_ple=w_ple.astype(BF16),
        w_ple_gate=w_ple_gate.astype(BF16), b_ple_gate=row(b_ple_gate))
    y_prompt = _trunk(x_prompt, p_prompt, prm, depth, d)
    y_sample = _trunk(x_sample, p_sample, prm, depth, d)
    return (y_prompt, y_sample)
```
